```python
import math
import jax, jax.numpy as jnp
from jax import lax
import numpy as np

D_MODEL = 1024
BATCH = 2
SEQ = 8192
DEPTH = 2

CTX_LEN = 256
GRID_W = 64
N_HEADS = 8
QK_DIM = 64
V_DIM = 2 * QK_DIM
ATTN_W = N_HEADS * V_DIM
CONV_W = 1024
CONV_K = 3
D_FF = -(-8 * D_MODEL // (3 * 256)) * 256
ROPE_BASE = 10000.0
ROPE_PAIRS_PER_AXIS = QK_DIM // 4
ATTN_SCALE = QK_DIM ** -0.5
Q_BLOCK = 128
EPS = 1e-6
N_MOD = 6

OFF_K = N_HEADS * 2 * QK_DIM
OFF_V = OFF_K + N_HEADS * 2 * QK_DIM
OFF_CX = OFF_V + ATTN_W
OFF_CB = OFF_CX + CONV_W
OFF_CC = OFF_CB + CONV_W
OFF_GA = OFF_CC + CONV_W
OFF_GC = OFF_GA + D_MODEL
N_IN = OFF_GC + D_MODEL
SPLITS = (OFF_K, OFF_V, OFF_CX, OFF_CB, OFF_CC, OFF_GA, OFF_GC)

kernel_name = "hybrid_diffattn_shortconv_dit"


def rms_norm(x, g):
    xf = x.astype(jnp.float32)
    y = xf * lax.rsqrt(jnp.mean(xf * xf, axis=-1, keepdims=True) + EPS)
    return (y * g.astype(jnp.float32)).astype(x.dtype)


def modulate(h, shift, scale):
    return h * (1 + scale) + shift


def axial_rope(n_tokens):
    rows = n_tokens // GRID_W
    row = jnp.repeat(jnp.arange(rows, dtype=jnp.float32), GRID_W)
    col = jnp.tile(jnp.arange(GRID_W, dtype=jnp.float32), rows)
    inv_freq = ROPE_BASE ** (-jnp.arange(ROPE_PAIRS_PER_AXIS, dtype=jnp.float32) / ROPE_PAIRS_PER_AXIS)
    ang = jnp.concatenate([row[:, None] * inv_freq, col[:, None] * inv_freq], axis=-1)
    return jnp.cos(ang), jnp.sin(ang)


def apply_rope(t, cos, sin):
    t1, t2 = jnp.split(t, 2, axis=-1)
    cos = cos.astype(t.dtype)
    sin = sin.astype(t.dtype)
    return jnp.concatenate([t1 * cos - t2 * sin, t2 * cos + t1 * sin], axis=-1)


def split_qk_heads(t, g):
    b, n, _ = t.shape
    t = t.reshape(b, n, N_HEADS, 2, QK_DIM).transpose(0, 2, 3, 1, 4)
    return rms_norm(t, g)


def split_v_heads(v):
    b, n, _ = v.shape
    return v.reshape(b, n, N_HEADS, V_DIM).transpose(0, 2, 1, 3)


def diff_weights(s, lam):
    p = jax.nn.softmax(s.astype(jnp.float32), axis=-1)
    return p[:, :, 0] - lam * p[:, :, 1]


def diff_attn_latent(q, k_all, v_all, lam):
    b, h, _, n, d = q.shape
    nb = n // Q_BLOCK
    qb = q.reshape(b, h, 2, nb, Q_BLOCK, d).transpose(3, 0, 1, 2, 4, 5)

    def one_block(qblk):
        s = jnp.einsum("bhmqd,bhmkd->bhmqk", qblk, k_all) * ATTN_SCALE
        w = diff_weights(s, lam)
        return jnp.einsum("bhqk,bhkd->bhqd", w.astype(v_all.dtype), v_all)

    o = lax.map(one_block, qb)
    return o.transpose(1, 2, 0, 3, 4).reshape(b, h, n, V_DIM)


def diff_attn_ctx(q, k, v, lam):
    s = jnp.einsum("bhmqd,bhmkd->bhmqk", q, k) * ATTN_SCALE
    w = diff_weights(s, lam)
    return jnp.einsum("bhqk,bhkd->bhqd", w.astype(v.dtype), v)


def diff_head_out(o, g, lam_init):
    o = rms_norm(o, g) * (1 - lam_init)
    b, h, n, d = o.shape
    return o.transpose(0, 2, 1, 3).reshape(b, n, h * d)


def short_conv(u, w):
    up = jnp.pad(u, ((0, 0), (1, 1), (0, 0)))
    return up[:, :-2] * w[0] + up[:, 1:-1] * w[1] + up[:, 2:] * w[2]


def merge_branches(attn, cx, cb, cc, ga, gc, conv_w, w_pa, w_pc, w_o):
    y_conv = cb * short_conv(cc * cx, conv_w)
    ya = attn @ w_pa
    yc = y_conv @ w_pc
    return (jax.nn.sigmoid(ga) * ya + jax.nn.sigmoid(gc) * yc) @ w_o


def swiglu(h, wg, wu, wd):
    return (jax.nn.silu(h @ wg) * (h @ wu)) @ wd


def setup_inputs(seed: int = 0) -> dict:
    key = jax.random.key(seed)
    ks = jax.random.split(key, 24)
    f32 = jnp.float32
    nrm = lambda k, shape, s: jax.random.normal(k, shape, f32) * s
    return {
        "x": nrm(ks[0], (BATCH, SEQ, D_MODEL), 1.0),
        "c": nrm(ks[1], (BATCH, D_MODEL), 1.0),
        "ctx": nrm(ks[2], (BATCH, CTX_LEN, D_MODEL), 1.0),
        "c_ctx": nrm(ks[3], (D_MODEL,), 1.0),
        "w_ada": nrm(ks[4], (DEPTH, D_MODEL, N_MOD * D_MODEL), 0.5 * D_MODEL ** -0.5),
        "b_ada": nrm(ks[5], (DEPTH, N_MOD * D_MODEL), 0.02),
        "norm1_g": 1.0 + nrm(ks[6], (DEPTH, D_MODEL), 0.02),
        "norm2_g": 1.0 + nrm(ks[7], (DEPTH, D_MODEL), 0.02),
        "w_in": nrm(ks[8], (DEPTH, D_MODEL, N_IN), D_MODEL ** -0.5),
        "q_norm_g": 1.0 + nrm(ks[9], (DEPTH, QK_DIM), 0.02),
        "k_norm_g": 1.0 + nrm(ks[10], (DEPTH, QK_DIM), 0.02),
        "lambda_q1": nrm(ks[11], (DEPTH, QK_DIM), 0.1),
        "lambda_k1": nrm(ks[12], (DEPTH, QK_DIM), 0.1),
        "lambda_q2": nrm(ks[13], (DEPTH, QK_DIM), 0.1),
        "lambda_k2": nrm(ks[14], (DEPTH, QK_DIM), 0.1),
        "subln_g": 1.0 + nrm(ks[15], (DEPTH, V_DIM), 0.02),
        "conv_w": nrm(ks[16], (DEPTH, CONV_K, CONV_W), CONV_K ** -0.5),
        "w_pa": nrm(ks[17], (DEPTH, ATTN_W, D_MODEL), ATTN_W ** -0.5),
        "w_pc": nrm(ks[18], (DEPTH, CONV_W, D_MODEL), CONV_W ** -0.5),
        "w_o": nrm(ks[19], (DEPTH, D_MODEL, D_MODEL), D_MODEL ** -0.5),
        "w_ffn_gate": nrm(ks[20], (DEPTH, D_MODEL, D_FF), D_MODEL ** -0.5),
        "w_ffn_up": nrm(ks[21], (DEPTH, D_MODEL, D_FF), D_MODEL ** -0.5),
        "w_ffn_down": nrm(ks[22], (DEPTH, D_FF, D_MODEL), D_FF ** -0.5),
    }


def reference(x, c, ctx, c_ctx, w_ada, b_ada, norm1_g, norm2_g, w_in, q_norm_g, k_norm_g,
              lambda_q1, lambda_k1, lambda_q2, lambda_k2, subln_g, conv_w, w_pa, w_pc, w_o,
              w_ffn_gate, w_ffn_up, w_ffn_down):
    n_lat = x.shape[1]
    cos, sin = axial_rope(n_lat)
    for li in range(DEPTH):
        last = li == DEPTH - 1
        lam_init = 0.8 - 0.6 * math.exp(-0.3 * li)
        lam = (jnp.exp(jnp.sum((lambda_q1[li] * lambda_k1[li]).astype(jnp.float32)))
               - jnp.exp(jnp.sum((lambda_q2[li] * lambda_k2[li]).astype(jnp.float32)))
               + lam_init)

        mod = jax.nn.silu(c) @ w_ada[li] + b_ada[li]
        mod_c = jax.nn.silu(c_ctx) @ w_ada[li] + b_ada[li]
        sh1, sc1, g1, sh2, sc2, g2 = [m[:, None, :] for m in jnp.split(mod, N_MOD, axis=-1)]
        csh1, csc1, cg1, csh2, csc2, cg2 = jnp.split(mod_c, N_MOD, axis=-1)

        h = modulate(rms_norm(x, norm1_g[li]), sh1, sc1)
        hc = modulate(rms_norm(ctx, norm1_g[li]), csh1, csc1)
        q, k, v, cx, cb, cc, ga, gc = jnp.split(h @ w_in[li], SPLITS, axis=-1)
        if last:
            kc, vc = jnp.split(hc @ w_in[li][:, OFF_K:OFF_CX], [OFF_V - OFF_K], axis=-1)
        else:
            qc, kc, vc, cxc, cbc, ccc, gac, gcc = jnp.split(hc @ w_in[li], SPLITS, axis=-1)

        kc_h = split_qk_heads(kc, k_norm_g[li])
        vc_h = split_v_heads(vc)
        q_h = apply_rope(split_qk_heads(q, q_norm_g[li]), cos, sin)
        k_h = apply_rope(split_qk_heads(k, k_norm_g[li]), cos, sin)
        k_all = jnp.concatenate([kc_h, k_h], axis=3)
        v_all = jnp.concatenate([vc_h, split_v_heads(v)], axis=2)
        attn = diff_head_out(diff_attn_latent(q_h, k_all, v_all, lam), subln_g[li], lam_init)

        mix = merge_branches(attn, cx, cb, cc, ga, gc, conv_w[li], w_pa[li], w_pc[li], w_o[li])
        x = x + g1 * mix
        hf = modulate(rms_norm(x, norm2_g[li]), sh2, sc2)
        x = x + g2 * swiglu(hf, w_ffn_gate[li], w_ffn_up[li], w_ffn_down[li])

        if not last:
            qc_h = split_qk_heads(qc, q_norm_g[li])
            attn_c = diff_head_out(diff_attn_ctx(qc_h, kc_h, vc_h, lam), subln_g[li], lam_init)
            mix_c = merge_branches(attn_c, cxc, cbc, ccc, gac, gcc, conv_w[li],
                                   w_pa[li], w_pc[li], w_o[li])
            ctx = ctx + cg1 * mix_c
            hfc = modulate(rms_norm(ctx, norm2_g[li]), csh2, csc2)
            ctx = ctx + cg2 * swiglu(hfc, w_ffn_gate[li], w_ffn_up[li], w_ffn_down[li])
    return x
```

```python
import functools
import math

import jax
import jax.numpy as jnp
from jax import lax
from jax.experimental import pallas as pl
from jax.experimental.pallas import tpu as pltpu

D_MODEL = 1024
N_HEADS = 8
QK_DIM = 64
HALF = QK_DIM // 2
V_DIM = 2 * QK_DIM
ATTN_W = N_HEADS * V_DIM
N_ATT = 3 * ATTN_W
N_REST = 5 * D_MODEL
D_FF = 2816
GRID_W = 64
ROPE_BASE = 10000.0
ATTN_SCALE = QK_DIM ** -0.5
EPS = 1e-6
N_MOD = 6
MOD_ROWS = 8
HALO = 16

BF16 = jnp.bfloat16
F32 = jnp.float32

VMEM_LIMIT = 56 * 1024 * 1024


def _params(sem):
    return pltpu.CompilerParams(dimension_semantics=sem, vmem_limit_bytes=VMEM_LIMIT)


def _sigmoid(x):
    return 1.0 / (1.0 + jnp.exp(-x))


def _norm_mod(x, g, shift, scale):
    ms = jnp.mean(x * x, axis=-1, keepdims=True)
    y = x * lax.rsqrt(ms + EPS) * g
    return y * (1.0 + scale) + shift


def _adaln_kernel(c_ref, w_ref, b_ref, o_ref):
    c = c_ref[...]
    s = (c * _sigmoid(c)).astype(BF16)
    w = w_ref[0].astype(BF16)
    o_ref[0] = jnp.dot(s, w, preferred_element_type=F32) + b_ref[0]


def _adaln(cmat, w_ada, b_ada):
    depth, d, n = w_ada.shape
    tn = 1536
    return pl.pallas_call(
        _adaln_kernel,
        out_shape=jax.ShapeDtypeStruct((depth, MOD_ROWS, n), F32),
        grid=(depth, n // tn),
        in_specs=[
            pl.BlockSpec((MOD_ROWS, d), lambda l, j: (0, 0)),
            pl.BlockSpec((1, d, tn), lambda l, j: (l, 0, j)),
            pl.BlockSpec((1, 1, tn), lambda l, j: (l, 0, j)),
        ],
        out_specs=pl.BlockSpec((1, MOD_ROWS, tn), lambda l, j: (l, 0, j)),
        compiler_params=_params(("arbitrary", "arbitrary")),
        name="adaln",
    )(cmat, w_ada, b_ada.reshape(depth, 1, n))


def _proj_t_kernel(x_ref, g_ref, sh_ref, sc_ref, w_ref, gcol_ref, cos_ref, sin_ref,
                   o_ref, ht_ref):
    j = pl.program_id(2)

    @pl.when(j == 0)
    def _():
        h = _norm_mod(x_ref[0], g_ref[...], sh_ref[0], sc_ref[0])
        ht_ref[...] = h.T.astype(BF16)

    y = jnp.dot(w_ref[...], ht_ref[...], preferred_element_type=F32)

    @pl.when(j < 2)
    def _():
        c = cos_ref[...]
        s = sin_ref[...]
        for gi in range(ATTN_W // QK_DIM):
            r0 = gi * QK_DIM
            blk = y[r0:r0 + QK_DIM, :]
            ms = jnp.mean(blk * blk, axis=0, keepdims=True)
            blk = blk * lax.rsqrt(ms + EPS) * gcol_ref[r0:r0 + QK_DIM, :]
            t1 = blk[:HALF, :]
            t2 = blk[HALF:, :]
            o_ref[0, r0:r0 + HALF, :] = (t1 * c - t2 * s).astype(BF16)
            o_ref[0, r0 + HALF:r0 + QK_DIM, :] = (t2 * c + t1 * s).astype(BF16)

    @pl.when(j == 2)
    def _():
        o_ref[0] = y.astype(BF16)


def _proj_t(x, g, shift, scale, w_t, gcol, cos_t, sin_t, tm):
    b, n, d = x.shape
    return pl.pallas_call(
        _proj_t_kernel,
        out_shape=jax.ShapeDtypeStruct((b, N_ATT, n), BF16),
        grid=(b, n // tm, 3),
        in_specs=[
            pl.BlockSpec((1, tm, d), lambda bi, i, j: (bi, i, 0)),
            pl.BlockSpec((1, d), lambda bi, i, j: (0, 0)),
            pl.BlockSpec((1, 1, d), lambda bi, i, j: (bi, 0, 0)),
            pl.BlockSpec((1, 1, d), lambda bi, i, j: (bi, 0, 0)),
            pl.BlockSpec((ATTN_W, d), lambda bi, i, j: (j, 0)),
            pl.BlockSpec((ATTN_W, 1), lambda bi, i, j: (jnp.minimum(j, 1), 0)),
            pl.BlockSpec((HALF, tm), lambda bi, i, j: (0, i)),
            pl.BlockSpec((HALF, tm), lambda bi, i, j: (0, i)),
        ],
        out_specs=pl.BlockSpec((1, ATTN_W, tm), lambda bi, i, j: (bi, j, i)),
        scratch_shapes=[pltpu.VMEM((d, tm), BF16)],
        compiler_params=_params(("arbitrary", "arbitrary", "arbitrary")),
        name="proj_qkv_t",
    )(x, g, shift, scale, w_t, gcol, cos_t, sin_t)


def _proj_r_kernel(x_ref, g_ref, sh_ref, sc_ref, w_ref, o_ref, h_ref):
    @pl.when(pl.program_id(2) == 0)
    def _():
        h_ref[...] = _norm_mod(x_ref[0], g_ref[...], sh_ref[0], sc_ref[0]).astype(BF16)

    o_ref[0] = jnp.dot(h_ref[...], w_ref[...], preferred_element_type=F32).astype(BF16)


def _proj_r(x, g, shift, scale, w, tm):
    b, n, d = x.shape
    nout = w.shape[1]
    tn = D_MODEL
    return pl.pallas_call(
        _proj_r_kernel,
        out_shape=jax.ShapeDtypeStruct((b, n, nout), BF16),
        grid=(b, n // tm, nout // tn),
        in_specs=[
            pl.BlockSpec((1, tm, d), lambda bi, i, j: (bi, i, 0)),
            pl.BlockSpec((1, d), lambda bi, i, j: (0, 0)),
            pl.BlockSpec((1, 1, d), lambda bi, i, j: (bi, 0, 0)),
            pl.BlockSpec((1, 1, d), lambda bi, i, j: (bi, 0, 0)),
            pl.BlockSpec((d, tn), lambda bi, i, j: (0, j)),
        ],
        out_specs=pl.BlockSpec((1, tm, tn), lambda bi, i, j: (bi, i, j)),
        scratch_shapes=[pltpu.VMEM((tm, d), BF16)],
        compiler_params=_params(("arbitrary", "arbitrary", "arbitrary")),
        name="proj_rest",
    )(x, g, shift, scale, w)


def _attn_kernel(*refs, lam_init, n_lat_blocks):
    if n_lat_blocks:
        (q_ref, kc_ref, vc_ref, k_ref, v_ref, lam_ref, g_ref, o_ref,
         kc_rows, k_rows, v_blk, m_ref, l_ref, acc_ref) = refs
    else:
        (q_ref, kc_ref, vc_ref, lam_ref, g_ref, o_ref,
         kc_rows, m_ref, l_ref, acc_ref) = refs

    @pl.when(pl.program_id(2) == 0)
    def _():
        kc_rows[...] = kc_ref[0].T
        for jb in range(n_lat_blocks):
            tk = k_rows.shape[1]
            k_rows[jb] = k_ref[0, :, jb * tk:(jb + 1) * tk].T
            v_blk[jb] = v_ref[0, :, jb * tk:(jb + 1) * tk]

    qt = q_ref[0]
    row = lax.broadcasted_iota(jnp.int32, qt.shape, 0)
    zero = jnp.zeros_like(qt)
    qmaps = (jnp.where(row < QK_DIM, qt, zero), jnp.where(row >= QK_DIM, qt, zero))

    def step(kb, vtb, first):
        for mi in range(2):
            s = jnp.dot(kb, qmaps[mi], preferred_element_type=F32)
            mb = jnp.max(s, axis=0, keepdims=True)
            if first:
                p = jnp.exp(s - mb)
                m_ref[mi] = mb
                l_ref[mi] = jnp.sum(p, axis=0, keepdims=True)
                acc_ref[mi] = jnp.dot(vtb, p.astype(BF16), preferred_element_type=F32)
            else:
                m_old = m_ref[mi]
                m_new = jnp.maximum(m_old, mb)
                alpha = jnp.exp(m_old - m_new)
                p = jnp.exp(s - m_new)
                m_ref[mi] = m_new
                l_ref[mi] = alpha * l_ref[mi] + jnp.sum(p, axis=0, keepdims=True)
                acc_ref[mi] = alpha * acc_ref[mi] + jnp.dot(
                    vtb, p.astype(BF16), preferred_element_type=F32)

    step(kc_rows[...], vc_ref[0], True)
    if n_lat_blocks:
        def body(jb, carry):
            step(k_rows[jb], v_blk[jb], False)
            return carry
        lax.fori_loop(0, n_lat_blocks, body, 0)

    lp = lam_ref[...]
    lam = (jnp.exp(jnp.sum(lp[0:1] * lp[1:2], axis=-1, keepdims=True))
           - jnp.exp(jnp.sum(lp[2:3] * lp[3:4], axis=-1, keepdims=True)) + lam_init)
    o = acc_ref[0] / l_ref[0] - lam * (acc_ref[1] / l_ref[1])
    ms = jnp.mean(o * o, axis=0, keepdims=True)
    o = o * lax.rsqrt(ms + EPS) * g_ref[...] * (1.0 - lam_init)
    o_ref[0] = o.T.astype(BF16)


def _attention(q_t, ctx_t, lat_t, lam_p, g_col, lam_init, tq, tk):
    b, _, nq = q_t.shape
    nc = ctx_t.shape[2]
    h = N_HEADS
    n_lat_blocks = 0 if lat_t is None else lat_t.shape[2] // tk
    in_specs = [
        pl.BlockSpec((1, V_DIM, tq), lambda bi, hi, qi: (bi, hi, qi)),
        pl.BlockSpec((1, V_DIM, nc), lambda bi, hi, qi: (bi, h + hi, 0)),
        pl.BlockSpec((1, V_DIM, nc), lambda bi, hi, qi: (bi, 2 * h + hi, 0)),
    ]
    args = [q_t, ctx_t, ctx_t]
    scratch = [pltpu.VMEM((nc, V_DIM), BF16)]
    if n_lat_blocks:
        nl = lat_t.shape[2]
        in_specs += [
            pl.BlockSpec((1, V_DIM, nl), lambda bi, hi, qi: (bi, h + hi, 0)),
            pl.BlockSpec((1, V_DIM, nl), lambda bi, hi, qi: (bi, 2 * h + hi, 0)),
        ]
        args += [lat_t, lat_t]
        scratch += [pltpu.VMEM((n_lat_blocks, tk, V_DIM), BF16),
                    pltpu.VMEM((n_lat_blocks, V_DIM, tk), BF16)]
    in_specs += [
        pl.BlockSpec((4, QK_DIM), lambda bi, hi, qi: (0, 0)),
        pl.BlockSpec((V_DIM, 1), lambda bi, hi, qi: (0, 0)),
    ]
    args += [lam_p, g_col]
    scratch += [pltpu.VMEM((2, 1, tq), F32), pltpu.VMEM((2, 1, tq), F32),
                pltpu.VMEM((2, V_DIM, tq), F32)]
    return pl.pallas_call(
        functools.partial(_attn_kernel, lam_init=lam_init, n_lat_blocks=n_lat_blocks),
        out_shape=jax.ShapeDtypeStruct((b, nq, ATTN_W), BF16),
        grid=(b, h, nq // tq),
        in_specs=in_specs,
        out_specs=pl.BlockSpec((1, tq, V_DIM), lambda bi, hi, qi: (bi, qi, hi)),
        scratch_shapes=scratch,
        compiler_params=_params(("arbitrary", "arbitrary", "arbitrary")),
        name="diff_attn" if n_lat_blocks else "diff_attn_ctx",
    )(*args)


def _merge_kernel(attn_ref, cx_ref, cb_ref, cc_ref, ga_ref, gc_ref,
                  cxp_ref, ccp_ref, cxn_ref, ccn_ref, cw_ref,
                  wpa_ref, wpc_ref, wo_ref, x_ref, g1_ref, o_ref):
    i = pl.program_id(1)
    last = pl.num_programs(1) - 1
    tm = x_ref.shape[1]
    u = cc_ref[0].astype(F32) * cx_ref[0].astype(F32)
    prev = (ccp_ref[0, HALO - 1:HALO, :].astype(F32) * cxp_ref[0, HALO - 1:HALO, :].astype(F32))
    nxt = ccn_ref[0, 0:1, :].astype(F32) * cxn_ref[0, 0:1, :].astype(F32)
    prev = jnp.where(i == 0, 0.0, prev)
    nxt = jnp.where(i == last, 0.0, nxt)
    rid = lax.broadcasted_iota(jnp.int32, u.shape, 0)
    u_prev = jnp.where(rid == 0, prev, pltpu.roll(u, 1, 0))
    u_next = jnp.where(rid == tm - 1, nxt, pltpu.roll(u, tm - 1, 0))
    cw = cw_ref[...]
    conv = u_prev * cw[0:1] + u * cw[1:2] + u_next * cw[2:3]
    y_conv = (cb_ref[0].astype(F32) * conv).astype(BF16)
    ya = jnp.dot(attn_ref[0], wpa_ref[...], preferred_element_type=F32)
    yc = jnp.dot(y_conv, wpc_ref[...], preferred_element_type=F32)
    z = _sigmoid(ga_ref[0].astype(F32)) * ya + _sigmoid(gc_ref[0].astype(F32)) * yc
    mix = jnp.dot(z.astype(BF16), wo_ref[...], preferred_element_type=F32)
    o_ref[0] = x_ref[0] + g1_ref[0] * mix


def _merge(attn, rest, conv_w, w_pa, w_pc, w_o, x, g1, tm):
    b, n, d = x.shape
    nh = n // HALO
    th = tm // HALO
    col = lambda c: pl.BlockSpec((1, tm, d), lambda bi, i, c=c: (bi, i, c))
    prev = lambda c: pl.BlockSpec(
        (1, HALO, d), lambda bi, i, c=c: (bi, jnp.maximum(i * th - 1, 0), c))
    nxt = lambda c: pl.BlockSpec(
        (1, HALO, d), lambda bi, i, c=c: (bi, jnp.minimum((i + 1) * th, nh - 1), c))
    full = lambda r: pl.BlockSpec((r, d), lambda bi, i: (0, 0))
    return pl.pallas_call(
        _merge_kernel,
        out_shape=jax.ShapeDtypeStruct((b, n, d), F32),
        grid=(b, n // tm),
        in_specs=[
            pl.BlockSpec((1, tm, d), lambda bi, i: (bi, i, 0)),
            col(0), col(1), col(2), col(3), col(4),
            prev(0), prev(2), nxt(0), nxt(2),
            full(3), full(d), full(d), full(d),
            pl.BlockSpec((1, tm, d), lambda bi, i: (bi, i, 0)),
            pl.BlockSpec((1, 1, d), lambda bi, i: (bi, 0, 0)),
        ],
        out_specs=pl.BlockSpec((1, tm, d), lambda bi, i: (bi, i, 0)),
        compiler_params=_params(("arbitrary", "arbitrary")),
        name="merge",
    )(attn, rest, rest, rest, rest, rest, rest, rest, rest, rest, conv_w,
      w_pa, w_pc, w_o, x, g1)


def _ffn_kernel(x_ref, g_ref, sh_ref, sc_ref, g2_ref, wg_ref, wu_ref, wd_ref,
                o_ref, h_ref, acc_ref):
    f = pl.program_id(2)

    @pl.when(f == 0)
    def _():
        h_ref[...] = _norm_mod(x_ref[0], g_ref[...], sh_ref[0], sc_ref[0]).astype(BF16)

    h = h_ref[...]
    gate = jnp.dot(h, wg_ref[...], preferred_element_type=F32)
    up = jnp.dot(h, wu_ref[...], preferred_element_type=F32)
    a = (gate * _sigmoid(gate) * up).astype(BF16)
    part = jnp.dot(a, wd_ref[...], preferred_element_type=F32)

    @pl.when(f == 0)
    def _():
        acc_ref[...] = part

    @pl.when(f > 0)
    def _():
        acc_ref[...] += part

    @pl.when(f == pl.num_programs(2) - 1)
    def _():
        o_ref[0] = x_ref[0] + g2_ref[0] * acc_ref[...]


def _ffn(x, g, shift, scale, g2, wg, wu, wd, tm, tf):
    b, n, d = x.shape
    dff = wg.shape[1]
    vec = pl.BlockSpec((1, 1, d), lambda bi, i, f: (bi, 0, 0))
    return pl.pallas_call(
        _ffn_kernel,
        out_shape=jax.ShapeDtypeStruct((b, n, d), F32),
        grid=(b, n // tm, dff // tf),
        in_specs=[
            pl.BlockSpec((1, tm, d), lambda bi, i, f: (bi, i, 0)),
            pl.BlockSpec((1, d), lambda bi, i, f: (0, 0)),
            vec, vec, vec,
            pl.BlockSpec((d, tf), lambda bi, i, f: (0, f)),
            pl.BlockSpec((d, tf), lambda bi, i, f: (0, f)),
            pl.BlockSpec((tf, d), lambda bi, i, f: (f, 0)),
        ],
        out_specs=pl.BlockSpec((1, tm, d), lambda bi, i, f: (bi, i, 0)),
        scratch_shapes=[pltpu.VMEM((tm, d), BF16), pltpu.VMEM((tm, d), F32)],
        compiler_params=_params(("arbitrary", "arbitrary", "arbitrary")),
        name="ffn",
    )(x, g, shift, scale, g2, wg, wu, wd)


def _rope_tables_t(n):
    pos = jnp.arange(n, dtype=jnp.int32)
    row = (pos // GRID_W).astype(F32)
    col = (pos % GRID_W).astype(F32)
    pairs = HALF // 2
    inv = ROPE_BASE ** (-jnp.arange(pairs, dtype=F32) / pairs)
    ang = jnp.concatenate([inv[:, None] * row[None, :], inv[:, None] * col[None, :]], axis=0)
    return jnp.cos(ang), jnp.sin(ang)


def _tile(n, pref):
    return pref if n % pref == 0 else n


@jax.jit
def _forward(x, c, ctx, c_ctx, w_ada, b_ada, norm1_g, norm2_g, w_in, q_norm_g, k_norm_g,
             lambda_q1, lambda_k1, lambda_q2, lambda_k2, subln_g, conv_w, w_pa, w_pc, w_o,
             w_ffn_gate, w_ffn_up, w_ffn_down):
    b, n, d = x.shape
    nc = ctx.shape[1]
    depth = w_ada.shape[0]

    cmat = jnp.concatenate([c, c_ctx[None, :]], axis=0)
    cmat = jnp.pad(cmat, ((0, MOD_ROWS - (b + 1)), (0, 0)))
    mod_all = _adaln(cmat, w_ada, b_ada)

    cos_t, sin_t = _rope_tables_t(n)
    cos_c = jnp.ones((HALF, nc), F32)
    sin_c = jnp.zeros((HALF, nc), F32)

    tm = _tile(n, 512)
    tmc = _tile(nc, 512)
    tq = _tile(n, 512)
    tk = _tile(n, 512)
    tf = D_FF // 2
    reps = ATTN_W // QK_DIM

    for li in range(depth):
        last = li == depth - 1
        lam_init = 0.8 - 0.6 * math.exp(-0.3 * li)
        mods = [m.reshape(b, 1, d) for m in jnp.split(mod_all[li, :b], N_MOD, axis=-1)]
        cmods = [jnp.broadcast_to(m.reshape(1, 1, d), (b, 1, d))
                 for m in jnp.split(mod_all[li, b], N_MOD, axis=-1)]
        sh1, sc1, g1, sh2, sc2, g2 = mods
        csh1, csc1, cg1, csh2, csc2, cg2 = cmods

        w_att_t = w_in[li][:, :N_ATT].T.astype(BF16)
        w_rest = w_in[li][:, N_ATT:].astype(BF16)
        gcol = jnp.concatenate([jnp.tile(q_norm_g[li], reps) * ATTN_SCALE,
                                jnp.tile(k_norm_g[li], reps)]).reshape(2 * ATTN_W, 1)
        lam_p = jnp.stack([lambda_q1[li], lambda_k1[li], lambda_q2[li], lambda_k2[li]])
        sub_g = subln_g[li].reshape(V_DIM, 1)
        n1 = norm1_g[li].reshape(1, d)
        n2 = norm2_g[li].reshape(1, d)
        wpa = w_pa[li].astype(BF16)
        wpc = w_pc[li].astype(BF16)
        wo = w_o[li].astype(BF16)
        wg = w_ffn_gate[li].astype(BF16)
        wu = w_ffn_up[li].astype(BF16)
        wd = w_ffn_down[li].astype(BF16)

        lat_t = _proj_t(x, n1, sh1, sc1, w_att_t, gcol, cos_t, sin_t, tm)
        rest = _proj_r(x, n1, sh1, sc1, w_rest, tm)
        ctx_t = _proj_t(ctx, n1, csh1, csc1, w_att_t, gcol, cos_c, sin_c, tmc)

        attn = _attention(lat_t, ctx_t, lat_t, lam_p, sub_g, lam_init, tq, tk)
        x = _merge(attn, rest, conv_w[li], wpa, wpc, wo, x, g1, tm)
        x = _ffn(x, n2, sh2, sc2, g2, wg, wu, wd, tm, tf)

        if not last:
            rest_c = _proj_r(ctx, n1, csh1, csc1, w_rest, tmc)
            attn_c = _attention(ctx_t, ctx_t, None, lam_p, sub_g, lam_init, tmc, tk)
            ctx = _merge(attn_c, rest_c, conv_w[li], wpa, wpc, wo, ctx, cg1, tmc)
            ctx = _ffn(ctx, n2, csh2, csc2, cg2, wg, wu, wd, tmc, tf)
    return x


def kernel(x, c, ctx, c_ctx, w_ada, b_ada, norm1_g, norm2_g, w_in, q_norm_g, k_norm_g,
           lambda_q1, lambda_k1, lambda_q2, lambda_k2, subln_g, conv_w, w_pa, w_pc, w_o,
           w_ffn_gate, w_ffn_up, w_ffn_down):
    return _forward(x, c, ctx, c_ctx, w_ada, b_ada, norm1_g, norm2_g, w_in, q_norm_g,
                    k_norm_g, lambda_q1, lambda_k1, lambda_q2, lambda_k2, subln_g, conv_w,
                    w_pa, w_pc, w_o, w_ffn_gate, w_ffn_up, w_ffn_down)
```

```python
import functools
import math

import jax
import jax.numpy as jnp
from jax import lax
from jax.experimental import pallas as pl
from jax.experimental.pallas import tpu as pltpu

D_MODEL = 1024
N_HEADS = 8
QK_DIM = 64
HALF = QK_DIM // 2
V_DIM = 2 * QK_DIM
V_AUG = V_DIM + 16
ATTN_W = N_HEADS * V_DIM
N_ATT = 3 * ATTN_W
N_REST = 5 * D_MODEL
D_FF = 2816
GRID_W = 64
ROPE_BASE = 10000.0
ATTN_SCALE = QK_DIM ** -0.5
LOG2E = math.log2(math.e)
EPS = 1e-6
N_MOD = 6
MOD_ROWS = 8
HALO = 16

BF16 = jnp.bfloat16
F32 = jnp.float32

VMEM_LIMIT = 56 * 1024 * 1024


def _params(sem):
    return pltpu.CompilerParams(dimension_semantics=sem, vmem_limit_bytes=VMEM_LIMIT)


def _sigmoid(x):
    return 1.0 / (1.0 + jnp.exp(-x))


def _norm_mod(x, g, shift, scale):
    ms = jnp.mean(x * x, axis=-1, keepdims=True)
    y = x * lax.rsqrt(ms + EPS) * g
    return y * (1.0 + scale) + shift


def _adaln_kernel(c_ref, w_ref, b_ref, o_ref):
    c = c_ref[...]
    s = (c * _sigmoid(c)).astype(BF16)
    w = w_ref[0].astype(BF16)
    o_ref[0] = jnp.dot(s, w, preferred_element_type=F32) + b_ref[0]


def _adaln(cmat, w_ada, b_ada):
    depth, d, n = w_ada.shape
    tn = 1536
    return pl.pallas_call(
        _adaln_kernel,
        out_shape=jax.ShapeDtypeStruct((depth, MOD_ROWS, n), F32),
        grid=(depth, n // tn),
        in_specs=[
            pl.BlockSpec((MOD_ROWS, d), lambda l, j: (0, 0)),
            pl.BlockSpec((1, d, tn), lambda l, j: (l, 0, j)),
            pl.BlockSpec((1, 1, tn), lambda l, j: (l, 0, j)),
        ],
        out_specs=pl.BlockSpec((1, MOD_ROWS, tn), lambda l, j: (l, 0, j)),
        compiler_params=_params(("arbitrary", "arbitrary")),
        name="adaln",
    )(cmat, w_ada, b_ada.reshape(depth, 1, n))


def _proj_t_kernel(x_ref, g_ref, sh_ref, sc_ref, w_ref, gcol_ref, cos_ref, sin_ref,
                   o_ref, ht_ref):
    j = pl.program_id(2)

    @pl.when(j == 0)
    def _():
        h = _norm_mod(x_ref[0], g_ref[...], sh_ref[0], sc_ref[0])
        ht_ref[...] = h.T.astype(BF16)

    y = jnp.dot(w_ref[...], ht_ref[...], preferred_element_type=F32)

    @pl.when(j < 2)
    def _():
        c = cos_ref[...]
        s = sin_ref[...]
        for gi in range(ATTN_W // QK_DIM):
            r0 = gi * QK_DIM
            blk = y[r0:r0 + QK_DIM, :]
            ms = jnp.mean(blk * blk, axis=0, keepdims=True)
            blk = blk * lax.rsqrt(ms + EPS) * gcol_ref[r0:r0 + QK_DIM, :]
            t1 = blk[:HALF, :]
            t2 = blk[HALF:, :]
            o_ref[0, r0:r0 + HALF, :] = (t1 * c - t2 * s).astype(BF16)
            o_ref[0, r0 + HALF:r0 + QK_DIM, :] = (t2 * c + t1 * s).astype(BF16)

    @pl.when(j == 2)
    def _():
        o_ref[0] = y.astype(BF16)


def _proj_t(x, g, shift, scale, w_t, gcol, cos_t, sin_t, tm):
    b, n, d = x.shape
    return pl.pallas_call(
        _proj_t_kernel,
        out_shape=jax.ShapeDtypeStruct((b, N_ATT, n), BF16),
        grid=(b, n // tm, 3),
        in_specs=[
            pl.BlockSpec((1, tm, d), lambda bi, i, j: (bi, i, 0)),
            pl.BlockSpec((1, d), lambda bi, i, j: (0, 0)),
            pl.BlockSpec((1, 1, d), lambda bi, i, j: (bi, 0, 0)),
            pl.BlockSpec((1, 1, d), lambda bi, i, j: (bi, 0, 0)),
            pl.BlockSpec((ATTN_W, d), lambda bi, i, j: (j, 0)),
            pl.BlockSpec((ATTN_W, 1), lambda bi, i, j: (jnp.minimum(j, 1), 0)),
            pl.BlockSpec((HALF, tm), lambda bi, i, j: (0, i)),
            pl.BlockSpec((HALF, tm), lambda bi, i, j: (0, i)),
        ],
        out_specs=pl.BlockSpec((1, ATTN_W, tm), lambda bi, i, j: (bi, j, i)),
        scratch_shapes=[pltpu.VMEM((d, tm), BF16)],
        compiler_params=_params(("arbitrary", "arbitrary", "arbitrary")),
        name="proj_qkv_t",
    )(x, g, shift, scale, w_t, gcol, cos_t, sin_t)


def _proj_r_kernel(x_ref, g_ref, sh_ref, sc_ref, w_ref, o_ref, h_ref):
    @pl.when(pl.program_id(2) == 0)
    def _():
        h_ref[...] = _norm_mod(x_ref[0], g_ref[...], sh_ref[0], sc_ref[0]).astype(BF16)

    o_ref[0] = jnp.dot(h_ref[...], w_ref[...], preferred_element_type=F32).astype(BF16)


def _proj_r(x, g, shift, scale, w, tm):
    b, n, d = x.shape
    nout = w.shape[1]
    tn = D_MODEL
    return pl.pallas_call(
        _proj_r_kernel,
        out_shape=jax.ShapeDtypeStruct((b, n, nout), BF16),
        grid=(b, n // tm, nout // tn),
        in_specs=[
            pl.BlockSpec((1, tm, d), lambda bi, i, j: (bi, i, 0)),
            pl.BlockSpec((1, d), lambda bi, i, j: (0, 0)),
            pl.BlockSpec((1, 1, d), lambda bi, i, j: (bi, 0, 0)),
            pl.BlockSpec((1, 1, d), lambda bi, i, j: (bi, 0, 0)),
            pl.BlockSpec((d, tn), lambda bi, i, j: (0, j)),
        ],
        out_specs=pl.BlockSpec((1, tm, tn), lambda bi, i, j: (bi, i, j)),
        scratch_shapes=[pltpu.VMEM((tm, d), BF16)],
        compiler_params=_params(("arbitrary", "arbitrary", "arbitrary")),
        name="proj_rest",
    )(x, g, shift, scale, w)


def _attn_kernel(*refs, lam_init, n_lat_blocks):
    if n_lat_blocks:
        (q_ref, kc_ref, vc_ref, k_ref, v_ref, lam_ref, g_ref, o_ref,
         kc_rows, vc_aug, k_rows, v_aug, qm_ref, s_buf, mb_buf, m_ref, acc_ref) = refs
    else:
        (q_ref, kc_ref, vc_ref, lam_ref, g_ref, o_ref,
         kc_rows, vc_aug, qm_ref, m_ref, acc_ref) = refs

    @pl.when(pl.program_id(2) == 0)
    def _():
        def ones_rows(n):
            r = lax.broadcasted_iota(jnp.int32, (V_AUG - V_DIM, n), 0)
            return jnp.where(r == 0, 1.0, 0.0).astype(BF16)

        kc_rows[...] = kc_ref[0].T
        vc_aug[:V_DIM, :] = vc_ref[0]
        vc_aug[V_DIM:, :] = ones_rows(vc_aug.shape[1])
        for jb in range(n_lat_blocks):
            tk = k_rows.shape[1]
            k_rows[jb] = k_ref[0, :, jb * tk:(jb + 1) * tk].T
            v_aug[jb, :V_DIM, :] = v_ref[0, :, jb * tk:(jb + 1) * tk]
            v_aug[jb, V_DIM:, :] = ones_rows(tk)

    qt = q_ref[0]
    row = lax.broadcasted_iota(jnp.int32, qt.shape, 0)
    zero = jnp.zeros_like(qt)
    qm_ref[0] = jnp.where(row < QK_DIM, qt, zero)
    qm_ref[1] = jnp.where(row >= QK_DIM, qt, zero)

    kcb = kc_rows[...]
    vcb = vc_aug[...]
    for mi in range(2):
        s = jnp.dot(kcb, qm_ref[mi], preferred_element_type=F32)
        mb = jnp.max(s, axis=0, keepdims=True)
        m_ref[mi] = mb
        acc_ref[mi] = jnp.dot(vcb, jnp.exp2(s - mb).astype(BF16), preferred_element_type=F32)

    def produce(slot, jb):
        kb = k_rows[jb]
        for mi in range(2):
            s = jnp.dot(kb, qm_ref[mi], preferred_element_type=F32)
            s_buf[slot, mi] = s
            mb_buf[slot, mi] = jnp.max(s, axis=0, keepdims=True)

    def consume(slot, jb):
        vtb = v_aug[jb]
        for mi in range(2):
            m_old = m_ref[mi]
            m_new = jnp.maximum(m_old, mb_buf[slot, mi])
            alpha = jnp.exp2(m_old - m_new)
            p = jnp.exp2(s_buf[slot, mi] - m_new).astype(BF16)
            m_ref[mi] = m_new
            acc_ref[mi] = alpha * acc_ref[mi] + jnp.dot(vtb, p, preferred_element_type=F32)

    if n_lat_blocks:
        produce(0, 0)

        def body(jj, carry):
            produce(1, 2 * jj + 1)
            consume(0, 2 * jj)
            produce(0, jnp.minimum(2 * jj + 2, n_lat_blocks - 1))
            consume(1, 2 * jj + 1)
            return carry
        lax.fori_loop(0, n_lat_blocks // 2, body, 0, unroll=2)

    lp = lam_ref[...]
    lam = (jnp.exp(jnp.sum(lp[0:1] * lp[1:2], axis=-1, keepdims=True))
           - jnp.exp(jnp.sum(lp[2:3] * lp[3:4], axis=-1, keepdims=True)) + lam_init)
    a1 = acc_ref[0]
    a2 = acc_ref[1]
    o = (a1[:V_DIM] / a1[V_DIM:V_DIM + 1]
         - lam * (a2[:V_DIM] / a2[V_DIM:V_DIM + 1]))
    ms = jnp.mean(o * o, axis=0, keepdims=True)
    o = o * lax.rsqrt(ms + EPS) * g_ref[...] * (1.0 - lam_init)
    o_ref[0] = o.T.astype(BF16)


def _attention(q_t, ctx_t, lat_t, lam_p, g_col, lam_init, tq, tk):
    b, _, nq = q_t.shape
    nc = ctx_t.shape[2]
    h = N_HEADS
    n_lat_blocks = 0 if lat_t is None else lat_t.shape[2] // tk
    in_specs = [
        pl.BlockSpec((1, V_DIM, tq), lambda bi, hi, qi: (bi, hi, qi)),
        pl.BlockSpec((1, V_DIM, nc), lambda bi, hi, qi: (bi, h + hi, 0)),
        pl.BlockSpec((1, V_DIM, nc), lambda bi, hi, qi: (bi, 2 * h + hi, 0)),
    ]
    args = [q_t, ctx_t, ctx_t]
    scratch = [pltpu.VMEM((nc, V_DIM), BF16), pltpu.VMEM((V_AUG, nc), BF16)]
    if n_lat_blocks:
        assert n_lat_blocks % 2 == 0, "latent key tiles are consumed in pairs"
        nl = lat_t.shape[2]
        in_specs += [
            pl.BlockSpec((1, V_DIM, nl), lambda bi, hi, qi: (bi, h + hi, 0)),
            pl.BlockSpec((1, V_DIM, nl), lambda bi, hi, qi: (bi, 2 * h + hi, 0)),
        ]
        args += [lat_t, lat_t]
        scratch += [pltpu.VMEM((n_lat_blocks, tk, V_DIM), BF16),
                    pltpu.VMEM((n_lat_blocks, V_AUG, tk), BF16)]
    in_specs += [
        pl.BlockSpec((4, QK_DIM), lambda bi, hi, qi: (0, 0)),
        pl.BlockSpec((V_DIM, 1), lambda bi, hi, qi: (0, 0)),
    ]
    args += [lam_p, g_col]
    scratch += [pltpu.VMEM((2, V_DIM, tq), BF16)]
    if n_lat_blocks:
        scratch += [pltpu.VMEM((2, 2, tk, tq), F32), pltpu.VMEM((2, 2, 1, tq), F32)]
    scratch += [pltpu.VMEM((2, 1, tq), F32), pltpu.VMEM((2, V_AUG, tq), F32)]
    return pl.pallas_call(
        functools.partial(_attn_kernel, lam_init=lam_init, n_lat_blocks=n_lat_blocks),
        out_shape=jax.ShapeDtypeStruct((b, nq, ATTN_W), BF16),
        grid=(b, h, nq // tq),
        in_specs=in_specs,
        out_specs=pl.BlockSpec((1, tq, V_DIM), lambda bi, hi, qi: (bi, qi, hi)),
        scratch_shapes=scratch,
        compiler_params=_params(("arbitrary", "arbitrary", "arbitrary")),
        name="diff_attn" if n_lat_blocks else "diff_attn_ctx",
    )(*args)


def _merge_kernel(attn_ref, cx_ref, cb_ref, cc_ref, ga_ref, gc_ref,
                  cxp_ref, ccp_ref, cxn_ref, ccn_ref, cw_ref,
                  wpa_ref, wpc_ref, wo_ref, x_ref, g1_ref, o_ref):
    i = pl.program_id(1)
    last = pl.num_programs(1) - 1
    tm = x_ref.shape[1]
    u = cc_ref[0].astype(F32) * cx_ref[0].astype(F32)
    prev = (ccp_ref[0, HALO - 1:HALO, :].astype(F32) * cxp_ref[0, HALO - 1:HALO, :].astype(F32))
    nxt = ccn_ref[0, 0:1, :].astype(F32) * cxn_ref[0, 0:1, :].astype(F32)
    prev = jnp.where(i == 0, 0.0, prev)
    nxt = jnp.where(i == last, 0.0, nxt)
    rid = lax.broadcasted_iota(jnp.int32, u.shape, 0)
    u_prev = jnp.where(rid == 0, prev, pltpu.roll(u, 1, 0))
    u_next = jnp.where(rid == tm - 1, nxt, pltpu.roll(u, tm - 1, 0))
    cw = cw_ref[...]
    conv = u_prev * cw[0:1] + u * cw[1:2] + u_next * cw[2:3]
    y_conv = (cb_ref[0].astype(F32) * conv).astype(BF16)
    ya = jnp.dot(attn_ref[0], wpa_ref[...], preferred_element_type=F32)
    yc = jnp.dot(y_conv, wpc_ref[...], preferred_element_type=F32)
    z = _sigmoid(ga_ref[0].astype(F32)) * ya + _sigmoid(gc_ref[0].astype(F32)) * yc
    mix = jnp.dot(z.astype(BF16), wo_ref[...], preferred_element_type=F32)
    o_ref[0] = x_ref[0] + g1_ref[0] * mix


def _merge(attn, rest, conv_w, w_pa, w_pc, w_o, x, g1, tm):
    b, n, d = x.shape
    nh = n // HALO
    th = tm // HALO
    col = lambda c: pl.BlockSpec((1, tm, d), lambda bi, i, c=c: (bi, i, c))
    prev = lambda c: pl.BlockSpec(
        (1, HALO, d), lambda bi, i, c=c: (bi, jnp.maximum(i * th - 1, 0), c))
    nxt = lambda c: pl.BlockSpec(
        (1, HALO, d), lambda bi, i, c=c: (bi, jnp.minimum((i + 1) * th, nh - 1), c))
    full = lambda r: pl.BlockSpec((r, d), lambda bi, i: (0, 0))
    return pl.pallas_call(
        _merge_kernel,
        out_shape=jax.ShapeDtypeStruct((b, n, d), F32),
        grid=(b, n // tm),
        in_specs=[
            pl.BlockSpec((1, tm, d), lambda bi, i: (bi, i, 0)),
            col(0), col(1), col(2), col(3), col(4),
            prev(0), prev(2), nxt(0), nxt(2),
            full(3), full(d), full(d), full(d),
            pl.BlockSpec((1, tm, d), lambda bi, i: (bi, i, 0)),
            pl.BlockSpec((1, 1, d), lambda bi, i: (bi, 0, 0)),
        ],
        out_specs=pl.BlockSpec((1, tm, d), lambda bi, i: (bi, i, 0)),
        compiler_params=_params(("arbitrary", "arbitrary")),
        name="merge",
    )(attn, rest, rest, rest, rest, rest, rest, rest, rest, rest, conv_w,
      w_pa, w_pc, w_o, x, g1)


def _ffn_kernel(x_ref, g_ref, sh_ref, sc_ref, g2_ref, wg_ref, wu_ref, wd_ref,
                o_ref, h_ref, acc_ref):
    f = pl.program_id(2)

    @pl.when(f == 0)
    def _():
        h_ref[...] = _norm_mod(x_ref[0], g_ref[...], sh_ref[0], sc_ref[0]).astype(BF16)

    h = h_ref[...]
    gate = jnp.dot(h, wg_ref[...], preferred_element_type=F32)
    up = jnp.dot(h, wu_ref[...], preferred_element_type=F32)
    a = (gate * _sigmoid(gate) * up).astype(BF16)
    part = jnp.dot(a, wd_ref[...], preferred_element_type=F32)

    @pl.when(f == 0)
    def _():
        acc_ref[...] = part

    @pl.when(f > 0)
    def _():
        acc_ref[...] += part

    @pl.when(f == pl.num_programs(2) - 1)
    def _():
        o_ref[0] = x_ref[0] + g2_ref[0] * acc_ref[...]


def _ffn(x, g, shift, scale, g2, wg, wu, wd, tm, tf):
    b, n, d = x.shape
    dff = wg.shape[1]
    vec = pl.BlockSpec((1, 1, d), lambda bi, i, f: (bi, 0, 0))
    return pl.pallas_call(
        _ffn_kernel,
        out_shape=jax.ShapeDtypeStruct((b, n, d), F32),
        grid=(b, n // tm, dff // tf),
        in_specs=[
            pl.BlockSpec((1, tm, d), lambda bi, i, f: (bi, i, 0)),
            pl.BlockSpec((1, d), lambda bi, i, f: (0, 0)),
            vec, vec, vec,
            pl.BlockSpec((d, tf), lambda bi, i, f: (0, f)),
            pl.BlockSpec((d, tf), lambda bi, i, f: (0, f)),
            pl.BlockSpec((tf, d), lambda bi, i, f: (f, 0)),
        ],
        out_specs=pl.BlockSpec((1, tm, d), lambda bi, i, f: (bi, i, 0)),
        scratch_shapes=[pltpu.VMEM((tm, d), BF16), pltpu.VMEM((tm, d), F32)],
        compiler_params=_params(("arbitrary", "arbitrary", "arbitrary")),
        name="ffn",
    )(x, g, shift, scale, g2, wg, wu, wd)


def _rope_tables_t(n):
    pos = jnp.arange(n, dtype=jnp.int32)
    row = (pos // GRID_W).astype(F32)
    col = (pos % GRID_W).astype(F32)
    pairs = HALF // 2
    inv = ROPE_BASE ** (-jnp.arange(pairs, dtype=F32) / pairs)
    ang = jnp.concatenate([inv[:, None] * row[None, :], inv[:, None] * col[None, :]], axis=0)
    return jnp.cos(ang), jnp.sin(ang)


def _tile(n, pref):
    return pref if n % pref == 0 else n


@jax.jit
def _forward(x, c, ctx, c_ctx, w_ada, b_ada, norm1_g, norm2_g, w_in, q_norm_g, k_norm_g,
             lambda_q1, lambda_k1, lambda_q2, lambda_k2, subln_g, conv_w, w_pa, w_pc, w_o,
             w_ffn_gate, w_ffn_up, w_ffn_down):
    b, n, d = x.shape
    nc = ctx.shape[1]
    depth = w_ada.shape[0]

    cmat = jnp.concatenate([c, c_ctx[None, :]], axis=0)
    cmat = jnp.pad(cmat, ((0, MOD_ROWS - (b + 1)), (0, 0)))
    mod_all = _adaln(cmat, w_ada, b_ada)

    cos_t, sin_t = _rope_tables_t(n)
    cos_c = jnp.ones((HALF, nc), F32)
    sin_c = jnp.zeros((HALF, nc), F32)

    tm = _tile(n, 512)
    tmc = _tile(nc, 512)
    tq = _tile(n, 512)
    tk = _tile(n, 512)
    tf = D_FF // 2
    reps = ATTN_W // QK_DIM

    for li in range(depth):
        last = li == depth - 1
        lam_init = 0.8 - 0.6 * math.exp(-0.3 * li)
        mods = [m.reshape(b, 1, d) for m in jnp.split(mod_all[li, :b], N_MOD, axis=-1)]
        cmods = [jnp.broadcast_to(m.reshape(1, 1, d), (b, 1, d))
                 for m in jnp.split(mod_all[li, b], N_MOD, axis=-1)]
        sh1, sc1, g1, sh2, sc2, g2 = mods
        csh1, csc1, cg1, csh2, csc2, cg2 = cmods

        w_att_t = w_in[li][:, :N_ATT].T.astype(BF16)
        w_rest = w_in[li][:, N_ATT:].astype(BF16)
        gcol = jnp.concatenate([jnp.tile(q_norm_g[li], reps) * (ATTN_SCALE * LOG2E),
                                jnp.tile(k_norm_g[li], reps)]).reshape(2 * ATTN_W, 1)
        lam_p = jnp.stack([lambda_q1[li], lambda_k1[li], lambda_q2[li], lambda_k2[li]])
        sub_g = subln_g[li].reshape(V_DIM, 1)
        n1 = norm1_g[li].reshape(1, d)
        n2 = norm2_g[li].reshape(1, d)
        wpa = w_pa[li].astype(BF16)
        wpc = w_pc[li].astype(BF16)
        wo = w_o[li].astype(BF16)
        wg = w_ffn_gate[li].astype(BF16)
        wu = w_ffn_up[li].astype(BF16)
        wd = w_ffn_down[li].astype(BF16)

        lat_t = _proj_t(x, n1, sh1, sc1, w_att_t, gcol, cos_t, sin_t, tm)
        rest = _proj_r(x, n1, sh1, sc1, w_rest, tm)
        ctx_t = _proj_t(ctx, n1, csh1, csc1, w_att_t, gcol, cos_c, sin_c, tmc)

        attn = _attention(lat_t, ctx_t, lat_t, lam_p, sub_g, lam_init, tq, tk)
        x = _merge(attn, rest, conv_w[li], wpa, wpc, wo, x, g1, tm)
        x = _ffn(x, n2, sh2, sc2, g2, wg, wu, wd, tm, tf)

        if not last:
            rest_c = _proj_r(ctx, n1, csh1, csc1, w_rest, tmc)
            attn_c = _attention(ctx_t, ctx_t, None, lam_p, sub_g, lam_init, tmc, tk)
            ctx = _merge(attn_c, rest_c, conv_w[li], wpa, wpc, wo, ctx, cg1, tmc)
            ctx = _ffn(ctx, n2, csh2, csc2, cg2, wg, wu, wd, tmc, tf)
    return x


def kernel(x, c, ctx, c_ctx, w_ada, b_ada, norm1_g, norm2_g, w_in, q_norm_g, k_norm_g,
           lambda_q1, lambda_k1, lambda_q2, lambda_k2, subln_g, conv_w, w_pa, w_pc, w_o,
           w_ffn_gate, w_ffn_up, w_ffn_down):
    return _forward(x, c, ctx, c_ctx, w_ada, b_ada, norm1_g, norm2_g, w_in, q_norm_g,
                    k_norm_g, lambda_q1, lambda_k1, lambda_q2, lambda_k2, subln_g, conv_w,
                    w_pa, w_pc, w_o, w_ffn_gate, w_ffn_up, w_ffn_down)
```

```python
import functools
import math

import jax
import jax.numpy as jnp
from jax import lax
from jax.experimental import pallas as pl
from jax.experimental.pallas import tpu as pltpu

D_MODEL = 1024
N_HEADS = 8
QK_DIM = 64
HALF = QK_DIM // 2
V_DIM = 2 * QK_DIM
V_AUG = V_DIM + 16
ATTN_W = N_HEADS * V_DIM
N_ATT = 3 * ATTN_W
N_REST = 5 * D_MODEL
D_FF = 2816
GRID_W = 64
ROPE_BASE = 10000.0
ATTN_SCALE = QK_DIM ** -0.5
LOG2E = math.log2(math.e)
EPS = 1e-6
N_MOD = 6
MOD_ROWS = 8
HALO = 16

BF16 = jnp.bfloat16
F32 = jnp.float32

VMEM_LIMIT = 56 * 1024 * 1024


def _params(sem):
    return pltpu.CompilerParams(dimension_semantics=sem, vmem_limit_bytes=VMEM_LIMIT)


def _sigmoid(x):
    return 1.0 / (1.0 + jnp.exp(-x))


def _norm_mod(x, g, shift, scale):
    ms = jnp.mean(x * x, axis=-1, keepdims=True)
    y = x * lax.rsqrt(ms + EPS) * g
    return y * (1.0 + scale) + shift


def _adaln_kernel(c_ref, w_ref, b_ref, o_ref):
    c = c_ref[...]
    s = (c * _sigmoid(c)).astype(BF16)
    w = w_ref[0].astype(BF16)
    o_ref[0] = jnp.dot(s, w, preferred_element_type=F32) + b_ref[0]


def _adaln(cmat, w_ada, b_ada):
    depth, d, n = w_ada.shape
    tn = 1536
    return pl.pallas_call(
        _adaln_kernel,
        out_shape=jax.ShapeDtypeStruct((depth, MOD_ROWS, n), F32),
        grid=(depth, n // tn),
        in_specs=[
            pl.BlockSpec((MOD_ROWS, d), lambda l, j: (0, 0)),
            pl.BlockSpec((1, d, tn), lambda l, j: (l, 0, j)),
            pl.BlockSpec((1, 1, tn), lambda l, j: (l, 0, j)),
        ],
        out_specs=pl.BlockSpec((1, MOD_ROWS, tn), lambda l, j: (l, 0, j)),
        compiler_params=_params(("arbitrary", "arbitrary")),
        name="adaln",
    )(cmat, w_ada, b_ada.reshape(depth, 1, n))


def _proj_kernel(*refs, with_rest):
    if with_rest:
        (x_ref, g_ref, sh_ref, sc_ref, wt_ref, gcol_ref, cos_ref, sin_ref, wr_ref,
         ot_ref, or_ref) = refs
    else:
        x_ref, g_ref, sh_ref, sc_ref, wt_ref, gcol_ref, cos_ref, sin_ref, ot_ref = refs
    h = _norm_mod(x_ref[0], g_ref[...], sh_ref[0], sc_ref[0])
    ht = h.T.astype(BF16)
    c = cos_ref[...]
    s = sin_ref[...]
    for j in range(2):
        y = jnp.dot(wt_ref[j * ATTN_W:(j + 1) * ATTN_W, :], ht, preferred_element_type=F32)
        for gi in range(ATTN_W // QK_DIM):
            r0 = gi * QK_DIM
            blk = y[r0:r0 + QK_DIM, :]
            ms = jnp.mean(blk * blk, axis=0, keepdims=True)
            g0 = j * ATTN_W + r0
            blk = blk * lax.rsqrt(ms + EPS) * gcol_ref[g0:g0 + QK_DIM, :]
            t1 = blk[:HALF, :]
            t2 = blk[HALF:, :]
            ot_ref[0, g0:g0 + HALF, :] = (t1 * c - t2 * s).astype(BF16)
            ot_ref[0, g0 + HALF:g0 + QK_DIM, :] = (t2 * c + t1 * s).astype(BF16)
    ot_ref[0, 2 * ATTN_W:, :] = jnp.dot(
        wt_ref[2 * ATTN_W:, :], ht, preferred_element_type=F32).astype(BF16)
    if with_rest:
        hb = h.astype(BF16)
        for cb in range(N_REST // D_MODEL):
            cols = slice(cb * D_MODEL, (cb + 1) * D_MODEL)
            or_ref[0, :, cols] = jnp.dot(
                hb, wr_ref[:, cols], preferred_element_type=F32).astype(BF16)


def _resident(shape):
    return pl.BlockSpec(shape, lambda *_: (0,) * len(shape), pipeline_mode=pl.Buffered(1))


def _proj(x, g, shift, scale, w_t, gcol, cos_t, sin_t, w_rest, tm):
    b, n, d = x.shape
    with_rest = w_rest is not None
    in_specs = [
        pl.BlockSpec((1, tm, d), lambda bi, i: (bi, i, 0)),
        pl.BlockSpec((1, d), lambda bi, i: (0, 0)),
        pl.BlockSpec((1, 1, d), lambda bi, i: (bi, 0, 0)),
        pl.BlockSpec((1, 1, d), lambda bi, i: (bi, 0, 0)),
        _resident((N_ATT, d)),
        pl.BlockSpec((2 * ATTN_W, 1), lambda bi, i: (0, 0)),
        pl.BlockSpec((HALF, tm), lambda bi, i: (0, i)),
        pl.BlockSpec((HALF, tm), lambda bi, i: (0, i)),
    ]
    args = [x, g, shift, scale, w_t, gcol, cos_t, sin_t]
    out_shape = [jax.ShapeDtypeStruct((b, N_ATT, n), BF16)]
    out_specs = [pl.BlockSpec((1, N_ATT, tm), lambda bi, i: (bi, 0, i))]
    if with_rest:
        in_specs.append(_resident((d, N_REST)))
        args.append(w_rest)
        out_shape.append(jax.ShapeDtypeStruct((b, n, N_REST), BF16))
        out_specs.append(pl.BlockSpec((1, tm, N_REST), lambda bi, i: (bi, i, 0)))
    outs = pl.pallas_call(
        functools.partial(_proj_kernel, with_rest=with_rest),
        out_shape=out_shape,
        grid=(b, n // tm),
        in_specs=in_specs,
        out_specs=out_specs,
        compiler_params=_params(("arbitrary", "arbitrary")),
        name="proj" if with_rest else "proj_qkv",
    )(*args)
    return (outs[0], outs[1]) if with_rest else (outs[0], None)


def _attn_kernel(*refs, lam_init, n_lat_blocks):
    if n_lat_blocks:
        (q_ref, kc_ref, vc_ref, k_ref, v_ref, lam_ref, g_ref, o_ref,
         kc_rows, vc_aug, k_rows, v_aug, qm_ref, s_buf, mb_buf, m_ref, acc_ref) = refs
    else:
        (q_ref, kc_ref, vc_ref, lam_ref, g_ref, o_ref,
         kc_rows, vc_aug, qm_ref, m_ref, acc_ref) = refs

    @pl.when(pl.program_id(2) == 0)
    def _():
        def ones_rows(n):
            r = lax.broadcasted_iota(jnp.int32, (V_AUG - V_DIM, n), 0)
            return jnp.where(r == 0, 1.0, 0.0).astype(BF16)

        kc_rows[...] = kc_ref[0].T
        vc_aug[:V_DIM, :] = vc_ref[0]
        vc_aug[V_DIM:, :] = ones_rows(vc_aug.shape[1])
        for jb in range(n_lat_blocks):
            tk = k_rows.shape[1]
            k_rows[jb] = k_ref[0, :, jb * tk:(jb + 1) * tk].T
            v_aug[jb, :V_DIM, :] = v_ref[0, :, jb * tk:(jb + 1) * tk]
            v_aug[jb, V_DIM:, :] = ones_rows(tk)

    qt = q_ref[0]
    row = lax.broadcasted_iota(jnp.int32, qt.shape, 0)
    zero = jnp.zeros_like(qt)
    qm_ref[0] = jnp.where(row < QK_DIM, qt, zero)
    qm_ref[1] = jnp.where(row >= QK_DIM, qt, zero)

    kcb = kc_rows[...]
    vcb = vc_aug[...]
    for mi in range(2):
        s = jnp.dot(kcb, qm_ref[mi], preferred_element_type=F32)
        mb = jnp.max(s, axis=0, keepdims=True)
        m_ref[mi] = mb
        acc_ref[mi] = jnp.dot(vcb, jnp.exp2(s - mb).astype(BF16), preferred_element_type=F32)

    def produce(slot, jb):
        kb = k_rows[jb]
        for mi in range(2):
            s = jnp.dot(kb, qm_ref[mi], preferred_element_type=F32)
            s_buf[slot, mi] = s
            mb_buf[slot, mi] = jnp.max(s, axis=0, keepdims=True)

    def consume(slot, jb):
        vtb = v_aug[jb]
        for mi in range(2):
            m_old = m_ref[mi]
            m_new = jnp.maximum(m_old, mb_buf[slot, mi])
            alpha = jnp.exp2(m_old - m_new)
            p = jnp.exp2(s_buf[slot, mi] - m_new).astype(BF16)
            m_ref[mi] = m_new
            acc_ref[mi] = alpha * acc_ref[mi] + jnp.dot(vtb, p, preferred_element_type=F32)

    if n_lat_blocks:
        produce(0, 0)

        def body(jj, carry):
            produce(1, 2 * jj + 1)
            consume(0, 2 * jj)
            produce(0, jnp.minimum(2 * jj + 2, n_lat_blocks - 1))
            consume(1, 2 * jj + 1)
            return carry
        lax.fori_loop(0, n_lat_blocks // 2, body, 0, unroll=4)

    lp = lam_ref[...]
    lam = (jnp.exp(jnp.sum(lp[0:1] * lp[1:2], axis=-1, keepdims=True))
           - jnp.exp(jnp.sum(lp[2:3] * lp[3:4], axis=-1, keepdims=True)) + lam_init)
    a1 = acc_ref[0]
    a2 = acc_ref[1]
    o = (a1[:V_DIM] / a1[V_DIM:V_DIM + 1]
         - lam * (a2[:V_DIM] / a2[V_DIM:V_DIM + 1]))
    ms = jnp.mean(o * o, axis=0, keepdims=True)
    o = o * lax.rsqrt(ms + EPS) * g_ref[...] * (1.0 - lam_init)
    o_ref[0] = o.T.astype(BF16)


def _attention(q_t, ctx_t, lat_t, lam_p, g_col, lam_init, tq, tk):
    b, _, nq = q_t.shape
    nc = ctx_t.shape[2]
    h = N_HEADS
    n_lat_blocks = 0 if lat_t is None else lat_t.shape[2] // tk
    in_specs = [
        pl.BlockSpec((1, V_DIM, tq), lambda bi, hi, qi: (bi, hi, qi)),
        pl.BlockSpec((1, V_DIM, nc), lambda bi, hi, qi: (bi, h + hi, 0)),
        pl.BlockSpec((1, V_DIM, nc), lambda bi, hi, qi: (bi, 2 * h + hi, 0)),
    ]
    args = [q_t, ctx_t, ctx_t]
    scratch = [pltpu.VMEM((nc, V_DIM), BF16), pltpu.VMEM((V_AUG, nc), BF16)]
    if n_lat_blocks:
        assert n_lat_blocks % 2 == 0, "latent key tiles are consumed in pairs"
        nl = lat_t.shape[2]
        in_specs += [
            pl.BlockSpec((1, V_DIM, nl), lambda bi, hi, qi: (bi, h + hi, 0)),
            pl.BlockSpec((1, V_DIM, nl), lambda bi, hi, qi: (bi, 2 * h + hi, 0)),
        ]
        args += [lat_t, lat_t]
        scratch += [pltpu.VMEM((n_lat_blocks, tk, V_DIM), BF16),
                    pltpu.VMEM((n_lat_blocks, V_AUG, tk), BF16)]
    in_specs += [
        pl.BlockSpec((4, QK_DIM), lambda bi, hi, qi: (0, 0)),
        pl.BlockSpec((V_DIM, 1), lambda bi, hi, qi: (0, 0)),
    ]
    args += [lam_p, g_col]
    scratch += [pltpu.VMEM((2, V_DIM, tq), BF16)]
    if n_lat_blocks:
        scratch += [pltpu.VMEM((2, 2, tk, tq), F32), pltpu.VMEM((2, 2, 1, tq), F32)]
    scratch += [pltpu.VMEM((2, 1, tq), F32), pltpu.VMEM((2, V_AUG, tq), F32)]
    return pl.pallas_call(
        functools.partial(_attn_kernel, lam_init=lam_init, n_lat_blocks=n_lat_blocks),
        out_shape=jax.ShapeDtypeStruct((b, nq, ATTN_W), BF16),
        grid=(b, h, nq // tq),
        in_specs=in_specs,
        out_specs=pl.BlockSpec((1, tq, V_DIM), lambda bi, hi, qi: (bi, qi, hi)),
        scratch_shapes=scratch,
        compiler_params=_params(("arbitrary", "arbitrary", "arbitrary")),
        name="diff_attn" if n_lat_blocks else "diff_attn_ctx",
    )(*args)


def _merge_kernel(attn_ref, cx_ref, cb_ref, cc_ref, ga_ref, gc_ref,
                  cxp_ref, ccp_ref, cxn_ref, ccn_ref, cw_ref,
                  wpa_ref, wpc_ref, wo_ref, x_ref, g1_ref, o_ref):
    i = pl.program_id(1)
    last = pl.num_programs(1) - 1
    tm = x_ref.shape[1]
    u = cc_ref[0].astype(F32) * cx_ref[0].astype(F32)
    prev = (ccp_ref[0, HALO - 1:HALO, :].astype(F32) * cxp_ref[0, HALO - 1:HALO, :].astype(F32))
    nxt = ccn_ref[0, 0:1, :].astype(F32) * cxn_ref[0, 0:1, :].astype(F32)
    prev = jnp.where(i == 0, 0.0, prev)
    nxt = jnp.where(i == last, 0.0, nxt)
    rid = lax.broadcasted_iota(jnp.int32, u.shape, 0)
    u_prev = jnp.where(rid == 0, prev, pltpu.roll(u, 1, 0))
    u_next = jnp.where(rid == tm - 1, nxt, pltpu.roll(u, tm - 1, 0))
    cw = cw_ref[...]
    conv = u_prev * cw[0:1] + u * cw[1:2] + u_next * cw[2:3]
    y_conv = (cb_ref[0].astype(F32) * conv).astype(BF16)
    ya = jnp.dot(attn_ref[0], wpa_ref[...], preferred_element_type=F32)
    yc = jnp.dot(y_conv, wpc_ref[...], preferred_element_type=F32)
    z = _sigmoid(ga_ref[0].astype(F32)) * ya + _sigmoid(gc_ref[0].astype(F32)) * yc
    mix = jnp.dot(z.astype(BF16), wo_ref[...], preferred_element_type=F32)
    o_ref[0] = x_ref[0] + g1_ref[0] * mix


def _merge(attn, rest, conv_w, w_pa, w_pc, w_o, x, g1, tm):
    b, n, d = x.shape
    nh = n // HALO
    th = tm // HALO
    col = lambda c: pl.BlockSpec((1, tm, d), lambda bi, i, c=c: (bi, i, c))
    prev = lambda c: pl.BlockSpec(
        (1, HALO, d), lambda bi, i, c=c: (bi, jnp.maximum(i * th - 1, 0), c))
    nxt = lambda c: pl.BlockSpec(
        (1, HALO, d), lambda bi, i, c=c: (bi, jnp.minimum((i + 1) * th, nh - 1), c))
    full = lambda r: pl.BlockSpec((r, d), lambda bi, i: (0, 0))
    return pl.pallas_call(
        _merge_kernel,
        out_shape=jax.ShapeDtypeStruct((b, n, d), F32),
        grid=(b, n // tm),
        in_specs=[
            pl.BlockSpec((1, tm, d), lambda bi, i: (bi, i, 0)),
            col(0), col(1), col(2), col(3), col(4),
            prev(0), prev(2), nxt(0), nxt(2),
            full(3), full(d), full(d), full(d),
            pl.BlockSpec((1, tm, d), lambda bi, i: (bi, i, 0)),
            pl.BlockSpec((1, 1, d), lambda bi, i: (bi, 0, 0)),
        ],
        out_specs=pl.BlockSpec((1, tm, d), lambda bi, i: (bi, i, 0)),
        compiler_params=_params(("arbitrary", "arbitrary")),
        name="merge",
    )(attn, rest, rest, rest, rest, rest, rest, rest, rest, rest, conv_w,
      w_pa, w_pc, w_o, x, g1)


FF_CHUNKS = ((0, 1536), (1536, 1280))


def _ffn_kernel(x_ref, g_ref, sh_ref, sc_ref, g2_ref, wg_ref, wu_ref, wd_ref, o_ref):
    x = x_ref[0]
    h = _norm_mod(x, g_ref[...], sh_ref[0], sc_ref[0]).astype(BF16)
    acc = None
    for c0, cw in FF_CHUNKS:
        gate = jnp.dot(h, wg_ref[:, c0:c0 + cw], preferred_element_type=F32)
        up = jnp.dot(h, wu_ref[:, c0:c0 + cw], preferred_element_type=F32)
        a = (gate * _sigmoid(gate) * up).astype(BF16)
        part = jnp.dot(a, wd_ref[c0:c0 + cw, :], preferred_element_type=F32)
        acc = part if acc is None else acc + part
    o_ref[0] = x + g2_ref[0] * acc


def _ffn(x, g, shift, scale, g2, wg, wu, wd, tm):
    b, n, d = x.shape
    dff = wg.shape[1]
    assert sum(cw for _, cw in FF_CHUNKS) == dff
    vec = pl.BlockSpec((1, 1, d), lambda bi, i: (bi, 0, 0))
    return pl.pallas_call(
        _ffn_kernel,
        out_shape=jax.ShapeDtypeStruct((b, n, d), F32),
        grid=(b, n // tm),
        in_specs=[
            pl.BlockSpec((1, tm, d), lambda bi, i: (bi, i, 0)),
            pl.BlockSpec((1, d), lambda bi, i: (0, 0)),
            vec, vec, vec,
            _resident((d, dff)), _resident((d, dff)), _resident((dff, d)),
        ],
        out_specs=pl.BlockSpec((1, tm, d), lambda bi, i: (bi, i, 0)),
        compiler_params=_params(("arbitrary", "arbitrary")),
        name="ffn",
    )(x, g, shift, scale, g2, wg, wu, wd)


def _rope_tables_t(n):
    pos = jnp.arange(n, dtype=jnp.int32)
    row = (pos // GRID_W).astype(F32)
    col = (pos % GRID_W).astype(F32)
    pairs = HALF // 2
    inv = ROPE_BASE ** (-jnp.arange(pairs, dtype=F32) / pairs)
    ang = jnp.concatenate([inv[:, None] * row[None, :], inv[:, None] * col[None, :]], axis=0)
    return jnp.cos(ang), jnp.sin(ang)


def _tile(n, pref):
    return pref if n % pref == 0 else n


@jax.jit
def _forward(x, c, ctx, c_ctx, w_ada, b_ada, norm1_g, norm2_g, w_in, q_norm_g, k_norm_g,
             lambda_q1, lambda_k1, lambda_q2, lambda_k2, subln_g, conv_w, w_pa, w_pc, w_o,
             w_ffn_gate, w_ffn_up, w_ffn_down):
    b, n, d = x.shape
    nc = ctx.shape[1]
    depth = w_ada.shape[0]

    cmat = jnp.concatenate([c, c_ctx[None, :]], axis=0)
    cmat = jnp.pad(cmat, ((0, MOD_ROWS - (b + 1)), (0, 0)))
    mod_all = _adaln(cmat, w_ada, b_ada)

    cos_t, sin_t = _rope_tables_t(n)
    cos_c = jnp.ones((HALF, nc), F32)
    sin_c = jnp.zeros((HALF, nc), F32)

    tm = _tile(n, 512)
    tmc = _tile(nc, 512)
    tq = _tile(n, 512)
    tk = _tile(n, 512)
    reps = ATTN_W // QK_DIM

    for li in range(depth):
        last = li == depth - 1
        lam_init = 0.8 - 0.6 * math.exp(-0.3 * li)
        mods = [m.reshape(b, 1, d) for m in jnp.split(mod_all[li, :b], N_MOD, axis=-1)]
        cmods = [jnp.broadcast_to(m.reshape(1, 1, d), (b, 1, d))
                 for m in jnp.split(mod_all[li, b], N_MOD, axis=-1)]
        sh1, sc1, g1, sh2, sc2, g2 = mods
        csh1, csc1, cg1, csh2, csc2, cg2 = cmods

        w_att_t = w_in[li][:, :N_ATT].T.astype(BF16)
        w_rest = w_in[li][:, N_ATT:].astype(BF16)
        gcol = jnp.concatenate([jnp.tile(q_norm_g[li], reps) * (ATTN_SCALE * LOG2E),
                                jnp.tile(k_norm_g[li], reps)]).reshape(2 * ATTN_W, 1)
        lam_p = jnp.stack([lambda_q1[li], lambda_k1[li], lambda_q2[li], lambda_k2[li]])
        sub_g = subln_g[li].reshape(V_DIM, 1)
        n1 = norm1_g[li].reshape(1, d)
        n2 = norm2_g[li].reshape(1, d)
        wpa = w_pa[li].astype(BF16)
        wpc = w_pc[li].astype(BF16)
        wo = w_o[li].astype(BF16)
        wg = w_ffn_gate[li].astype(BF16)
        wu = w_ffn_up[li].astype(BF16)
        wd = w_ffn_down[li].astype(BF16)

        lat_t, rest = _proj(x, n1, sh1, sc1, w_att_t, gcol, cos_t, sin_t, w_rest, tm)
        ctx_t, rest_c = _proj(ctx, n1, csh1, csc1, w_att_t, gcol, cos_c, sin_c,
                              None if last else w_rest, tmc)

        attn = _attention(lat_t, ctx_t, lat_t, lam_p, sub_g, lam_init, tq, tk)
        x = _merge(attn, rest, conv_w[li], wpa, wpc, wo, x, g1, tm)
        x = _ffn(x, n2, sh2, sc2, g2, wg, wu, wd, tm)

        if not last:
            attn_c = _attention(ctx_t, ctx_t, None, lam_p, sub_g, lam_init, tmc, tk)
            ctx = _merge(attn_c, rest_c, conv_w[li], wpa, wpc, wo, ctx, cg1, tmc)
            ctx = _ffn(ctx, n2, csh2, csc2, cg2, wg, wu, wd, tmc)
    return x


def kernel(x, c, ctx, c_ctx, w_ada, b_ada, norm1_g, norm2_g, w_in, q_norm_g, k_norm_g,
           lambda_q1, lambda_k1, lambda_q2, lambda_k2, subln_g, conv_w, w_pa, w_pc, w_o,
           w_ffn_gate, w_ffn_up, w_ffn_down):
    return _forward(x, c, ctx, c_ctx, w_ada, b_ada, norm1_g, norm2_g, w_in, q_norm_g,
                    k_norm_g, lambda_q1, lambda_k1, lambda_q2, lambda_k2, subln_g, conv_w,
                    w_pa, w_pc, w_o, w_ffn_gate, w_ffn_up, w_ffn_down)
```

```python
import functools
import math

import jax
import jax.numpy as jnp
from jax import lax
from jax.experimental import pallas as pl
from jax.experimental.pallas import tpu as pltpu

D_MODEL = 1024
N_HEADS = 8
QK_DIM = 64
HALF = QK_DIM // 2
V_DIM = 2 * QK_DIM
V_AUG = V_DIM + 16
ATTN_W = N_HEADS * V_DIM
N_ATT = 3 * ATTN_W
N_REST = 5 * D_MODEL
D_FF = 2816
GRID_W = 64
ROPE_BASE = 10000.0
ATTN_SCALE = QK_DIM ** -0.5
LOG2E = math.log2(math.e)
EPS = 1e-6
N_MOD = 6
MOD_ROWS = 8
HALO = 16
N_SUB = 2

BF16 = jnp.bfloat16
F32 = jnp.float32

VMEM_LIMIT = 56 * 1024 * 1024


def _params(sem):
    return pltpu.CompilerParams(dimension_semantics=sem, vmem_limit_bytes=VMEM_LIMIT)


def _sigmoid(x):
    return 1.0 / (1.0 + jnp.exp(-x))


def _norm_mod(x, g, shift, scale):
    ms = jnp.mean(x * x, axis=-1, keepdims=True)
    y = x * lax.rsqrt(ms + EPS) * g
    return y * (1.0 + scale) + shift


def _adaln_kernel(c_ref, w_ref, b_ref, o_ref):
    c = c_ref[...]
    s = (c * _sigmoid(c)).astype(BF16)
    w = w_ref[0].astype(BF16)
    o_ref[0] = jnp.dot(s, w, preferred_element_type=F32) + b_ref[0]


def _adaln(cmat, w_ada, b_ada):
    depth, d, n = w_ada.shape
    tn = 1536
    return pl.pallas_call(
        _adaln_kernel,
        out_shape=jax.ShapeDtypeStruct((depth, MOD_ROWS, n), F32),
        grid=(depth, n // tn),
        in_specs=[
            pl.BlockSpec((MOD_ROWS, d), lambda l, j: (0, 0)),
            pl.BlockSpec((1, d, tn), lambda l, j: (l, 0, j)),
            pl.BlockSpec((1, 1, tn), lambda l, j: (l, 0, j)),
        ],
        out_specs=pl.BlockSpec((1, MOD_ROWS, tn), lambda l, j: (l, 0, j)),
        compiler_params=_params(("arbitrary", "arbitrary")),
        name="adaln",
    )(cmat, w_ada, b_ada.reshape(depth, 1, n))


def _proj_kernel(*refs, with_rest):
    if with_rest:
        (x_ref, g_ref, sh_ref, sc_ref, wt_ref, gcol_ref, cos_ref, sin_ref, wr_ref,
         ot_ref, or_ref) = refs
    else:
        x_ref, g_ref, sh_ref, sc_ref, wt_ref, gcol_ref, cos_ref, sin_ref, ot_ref = refs
    c_all = cos_ref[...]
    s_all = sin_ref[...]
    tm = x_ref.shape[1]
    ts = tm // N_SUB
    for si in range(N_SUB):
        rows = slice(si * ts, (si + 1) * ts)
        h = _norm_mod(x_ref[0, rows, :], g_ref[...], sh_ref[0], sc_ref[0])
        ht = h.T.astype(BF16)
        c = c_all[:, rows]
        s = s_all[:, rows]
        for j in range(2):
            y = jnp.dot(wt_ref[j * ATTN_W:(j + 1) * ATTN_W, :], ht,
                        preferred_element_type=F32)
            for gi in range(ATTN_W // QK_DIM):
                r0 = gi * QK_DIM
                blk = y[r0:r0 + QK_DIM, :]
                ms = jnp.mean(blk * blk, axis=0, keepdims=True)
                g0 = j * ATTN_W + r0
                blk = blk * lax.rsqrt(ms + EPS) * gcol_ref[g0:g0 + QK_DIM, :]
                t1 = blk[:HALF, :]
                t2 = blk[HALF:, :]
                ot_ref[0, g0:g0 + HALF, rows] = (t1 * c - t2 * s).astype(BF16)
                ot_ref[0, g0 + HALF:g0 + QK_DIM, rows] = (t2 * c + t1 * s).astype(BF16)
        ot_ref[0, 2 * ATTN_W:, rows] = jnp.dot(
            wt_ref[2 * ATTN_W:, :], ht, preferred_element_type=F32).astype(BF16)
        if with_rest:
            hb = h.astype(BF16)
            for cb in range(N_REST // D_MODEL):
                cols = slice(cb * D_MODEL, (cb + 1) * D_MODEL)
                or_ref[0, rows, cols] = jnp.dot(
                    hb, wr_ref[:, cols], preferred_element_type=F32).astype(BF16)


def _resident(shape):
    return pl.BlockSpec(shape, lambda *_: (0,) * len(shape), pipeline_mode=pl.Buffered(1))


def _proj(x, g, shift, scale, w_t, gcol, cos_t, sin_t, w_rest, tm):
    b, n, d = x.shape
    with_rest = w_rest is not None
    in_specs = [
        pl.BlockSpec((1, tm, d), lambda bi, i: (bi, i, 0)),
        pl.BlockSpec((1, d), lambda bi, i: (0, 0)),
        pl.BlockSpec((1, 1, d), lambda bi, i: (bi, 0, 0)),
        pl.BlockSpec((1, 1, d), lambda bi, i: (bi, 0, 0)),
        _resident((N_ATT, d)),
        pl.BlockSpec((2 * ATTN_W, 1), lambda bi, i: (0, 0)),
        pl.BlockSpec((HALF, tm), lambda bi, i: (0, i)),
        pl.BlockSpec((HALF, tm), lambda bi, i: (0, i)),
    ]
    args = [x, g, shift, scale, w_t, gcol, cos_t, sin_t]
    out_shape = [jax.ShapeDtypeStruct((b, N_ATT, n), BF16)]
    out_specs = [pl.BlockSpec((1, N_ATT, tm), lambda bi, i: (bi, 0, i))]
    if with_rest:
        in_specs.append(_resident((d, N_REST)))
        args.append(w_rest)
        out_shape.append(jax.ShapeDtypeStruct((b, n, N_REST), BF16))
        out_specs.append(pl.BlockSpec((1, tm, N_REST), lambda bi, i: (bi, i, 0)))
    outs = pl.pallas_call(
        functools.partial(_proj_kernel, with_rest=with_rest),
        out_shape=out_shape,
        grid=(b, n // tm),
        in_specs=in_specs,
        out_specs=out_specs,
        compiler_params=_params(("arbitrary", "arbitrary")),
        name="proj" if with_rest else "proj_qkv",
    )(*args)
    return (outs[0], outs[1]) if with_rest else (outs[0], None)


def _attn_kernel(*refs, lam_init, n_lat_blocks):
    nb = n_lat_blocks
    if nb:
        (q_ref, qn_ref, kc_ref, vc_ref, k_ref, v_ref, lam_ref, g_ref, o_ref,
         kc_rows, vc_aug, k_rows, v_aug, qm_ref, sc_buf, mbc_buf, s_buf, mb_buf,
         m_ref, acc_ref) = refs
    else:
        (q_ref, kc_ref, vc_ref, lam_ref, g_ref, o_ref,
         kc_rows, vc_aug, qm_ref, sc_buf, mbc_buf, m_ref, acc_ref) = refs
    qi = pl.program_id(2)
    par = qi % 2 if nb else 0

    def build_qm(slot, src_ref):
        qt = src_ref[0]
        row = lax.broadcasted_iota(jnp.int32, qt.shape, 0)
        zero = jnp.zeros_like(qt)
        qm_ref[slot, 0] = jnp.where(row < QK_DIM, qt, zero)
        qm_ref[slot, 1] = jnp.where(row >= QK_DIM, qt, zero)

    def produce_ctx(qslot):
        kcb = kc_rows[...]
        for mi in range(2):
            s = jnp.dot(kcb, qm_ref[qslot, mi], preferred_element_type=F32)
            sc_buf[mi] = s
            mbc_buf[mi] = jnp.max(s, axis=0, keepdims=True)

    def produce(slot, jb, qslot):
        kb = k_rows[jb]
        for mi in range(2):
            s = jnp.dot(kb, qm_ref[qslot, mi], preferred_element_type=F32)
            s_buf[slot, mi] = s
            mb_buf[slot, mi] = jnp.max(s, axis=0, keepdims=True)

    def consume_ctx():
        vcb = vc_aug[...]
        for mi in range(2):
            mb = mbc_buf[mi]
            m_ref[mi] = mb
            acc_ref[mi] = jnp.dot(vcb, jnp.exp2(sc_buf[mi] - mb).astype(BF16),
                                  preferred_element_type=F32)

    def consume(slot, jb):
        vtb = v_aug[jb]
        for mi in range(2):
            m_old = m_ref[mi]
            m_new = jnp.maximum(m_old, mb_buf[slot, mi])
            alpha = jnp.exp2(m_old - m_new)
            p = jnp.exp2(s_buf[slot, mi] - m_new).astype(BF16)
            m_ref[mi] = m_new
            acc_ref[mi] = alpha * acc_ref[mi] + jnp.dot(vtb, p, preferred_element_type=F32)

    @pl.when(qi == 0)
    def _():
        def ones_rows(n):
            r = lax.broadcasted_iota(jnp.int32, (V_AUG - V_DIM, n), 0)
            return jnp.where(r == 0, 1.0, 0.0).astype(BF16)

        kc_rows[...] = kc_ref[0].T
        vc_aug[:V_DIM, :] = vc_ref[0]
        vc_aug[V_DIM:, :] = ones_rows(vc_aug.shape[1])
        for jb in range(nb):
            tk = k_rows.shape[1]
            k_rows[jb] = k_ref[0, :, jb * tk:(jb + 1) * tk].T
            v_aug[jb, :V_DIM, :] = v_ref[0, :, jb * tk:(jb + 1) * tk]
            v_aug[jb, V_DIM:, :] = ones_rows(tk)
        build_qm(0, q_ref)
        produce_ctx(0)
        if nb:
            produce(2, 0, 0)

    if nb:
        build_qm(1 - par, qn_ref)
    consume_ctx()

    if nb:
        def slot_of(jb):
            return 2 if jb == 0 else jb % 2

        def emit(jb):
            if jb + 1 < nb:
                produce(slot_of(jb + 1), jb + 1, par)
            else:
                produce_ctx(1 - par)
                produce(2, 0, 1 - par)
            consume(slot_of(jb), jb)

        n_pre = 2 if nb >= 4 else 0
        n_pairs = (nb - n_pre - 2) // 2
        for jb in range(n_pre):
            emit(jb)
        if n_pairs:
            def body(pi, carry):
                j0 = n_pre + 2 * pi
                produce(1, j0 + 1, par)
                consume(0, j0)
                produce(0, j0 + 2, par)
                consume(1, j0 + 1)
                return carry
            unroll = 3 if n_pairs % 3 == 0 else (2 if n_pairs % 2 == 0 else 1)
            lax.fori_loop(0, n_pairs, body, 0, unroll=unroll)
        for jb in range(n_pre + 2 * n_pairs, nb):
            emit(jb)

    lp = lam_ref[...]
    lam = (jnp.exp(jnp.sum(lp[0:1] * lp[1:2], axis=-1, keepdims=True))
           - jnp.exp(jnp.sum(lp[2:3] * lp[3:4], axis=-1, keepdims=True)) + lam_init)
    a1 = acc_ref[0]
    a2 = acc_ref[1]
    o = (a1[:V_DIM] / a1[V_DIM:V_DIM + 1]
         - lam * (a2[:V_DIM] / a2[V_DIM:V_DIM + 1]))
    ms = jnp.mean(o * o, axis=0, keepdims=True)
    o = o * lax.rsqrt(ms + EPS) * g_ref[...] * (1.0 - lam_init)
    o_ref[0] = o.T.astype(BF16)


def _attention(q_t, ctx_t, lat_t, lam_p, g_col, lam_init, tq, tk):
    b, _, nq = q_t.shape
    nc = ctx_t.shape[2]
    h = N_HEADS
    n_lat_blocks = 0 if lat_t is None else lat_t.shape[2] // tk
    n_q = nq // tq
    in_specs = [pl.BlockSpec((1, V_DIM, tq), lambda bi, hi, qi: (bi, hi, qi))]
    args = [q_t]
    if n_lat_blocks:
        in_specs.append(pl.BlockSpec(
            (1, V_DIM, tq), lambda bi, hi, qi: (bi, hi, jnp.minimum(qi + 1, n_q - 1))))
        args.append(q_t)
    in_specs += [
        pl.BlockSpec((1, V_DIM, nc), lambda bi, hi, qi: (bi, h + hi, 0)),
        pl.BlockSpec((1, V_DIM, nc), lambda bi, hi, qi: (bi, 2 * h + hi, 0)),
    ]
    args += [ctx_t, ctx_t]
    scratch = [pltpu.VMEM((nc, V_DIM), BF16), pltpu.VMEM((V_AUG, nc), BF16)]
    if n_lat_blocks:
        assert n_lat_blocks % 2 == 0, "latent key tiles are consumed in pairs"
        nl = lat_t.shape[2]
        in_specs += [
            pl.BlockSpec((1, V_DIM, nl), lambda bi, hi, qi: (bi, h + hi, 0)),
            pl.BlockSpec((1, V_DIM, nl), lambda bi, hi, qi: (bi, 2 * h + hi, 0)),
        ]
        args += [lat_t, lat_t]
        scratch += [pltpu.VMEM((n_lat_blocks, tk, V_DIM), BF16),
                    pltpu.VMEM((n_lat_blocks, V_AUG, tk), BF16)]
    in_specs += [
        pl.BlockSpec((4, QK_DIM), lambda bi, hi, qi: (0, 0)),
        pl.BlockSpec((V_DIM, 1), lambda bi, hi, qi: (0, 0)),
    ]
    args += [lam_p, g_col]
    scratch += [pltpu.VMEM((2, 2, V_DIM, tq), BF16),
                pltpu.VMEM((2, nc, tq), F32), pltpu.VMEM((2, 1, tq), F32)]
    if n_lat_blocks:
        scratch += [pltpu.VMEM((3, 2, tk, tq), F32), pltpu.VMEM((3, 2, 1, tq), F32)]
    scratch += [pltpu.VMEM((2, 1, tq), F32), pltpu.VMEM((2, V_AUG, tq), F32)]
    return pl.pallas_call(
        functools.partial(_attn_kernel, lam_init=lam_init, n_lat_blocks=n_lat_blocks),
        out_shape=jax.ShapeDtypeStruct((b, nq, ATTN_W), BF16),
        grid=(b, h, n_q),
        in_specs=in_specs,
        out_specs=pl.BlockSpec((1, tq, V_DIM), lambda bi, hi, qi: (bi, qi, hi)),
        scratch_shapes=scratch,
        compiler_params=_params(("arbitrary", "arbitrary", "arbitrary")),
        name="diff_attn" if n_lat_blocks else "diff_attn_ctx",
    )(*args)


def _merge_kernel(attn_ref, cx_ref, cb_ref, cc_ref, ga_ref, gc_ref,
                  cxp_ref, ccp_ref, cxn_ref, ccn_ref, cw_ref,
                  wpa_ref, wpc_ref, wo_ref, x_ref, g1_ref, o_ref):
    i = pl.program_id(1)
    last = pl.num_programs(1) - 1
    tm = x_ref.shape[1]
    u = cc_ref[0].astype(F32) * cx_ref[0].astype(F32)
    prev = (ccp_ref[0, HALO - 1:HALO, :].astype(F32) * cxp_ref[0, HALO - 1:HALO, :].astype(F32))
    nxt = ccn_ref[0, 0:1, :].astype(F32) * cxn_ref[0, 0:1, :].astype(F32)
    prev = jnp.where(i == 0, 0.0, prev)
    nxt = jnp.where(i == last, 0.0, nxt)
    rid = lax.broadcasted_iota(jnp.int32, u.shape, 0)
    u_prev = jnp.where(rid == 0, prev, pltpu.roll(u, 1, 0))
    u_next = jnp.where(rid == tm - 1, nxt, pltpu.roll(u, tm - 1, 0))
    cw = cw_ref[...]
    conv = u_prev * cw[0:1] + u * cw[1:2] + u_next * cw[2:3]
    y_conv = (cb_ref[0].astype(F32) * conv).astype(BF16)
    ya = jnp.dot(attn_ref[0], wpa_ref[...], preferred_element_type=F32)
    yc = jnp.dot(y_conv, wpc_ref[...], preferred_element_type=F32)
    z = _sigmoid(ga_ref[0].astype(F32)) * ya + _sigmoid(gc_ref[0].astype(F32)) * yc
    mix = jnp.dot(z.astype(BF16), wo_ref[...], preferred_element_type=F32)
    o_ref[0] = x_ref[0] + g1_ref[0] * mix


def _merge(attn, rest, conv_w, w_pa, w_pc, w_o, x, g1, tm):
    b, n, d = x.shape
    nh = n // HALO
    th = tm // HALO
    col = lambda c: pl.BlockSpec((1, tm, d), lambda bi, i, c=c: (bi, i, c))
    prev = lambda c: pl.BlockSpec(
        (1, HALO, d), lambda bi, i, c=c: (bi, jnp.maximum(i * th - 1, 0), c))
    nxt = lambda c: pl.BlockSpec(
        (1, HALO, d), lambda bi, i, c=c: (bi, jnp.minimum((i + 1) * th, nh - 1), c))
    full = lambda r: pl.BlockSpec((r, d), lambda bi, i: (0, 0))
    return pl.pallas_call(
        _merge_kernel,
        out_shape=jax.ShapeDtypeStruct((b, n, d), F32),
        grid=(b, n // tm),
        in_specs=[
            pl.BlockSpec((1, tm, d), lambda bi, i: (bi, i, 0)),
            col(0), col(1), col(2), col(3), col(4),
            prev(0), prev(2), nxt(0), nxt(2),
            full(3), full(d), full(d), full(d),
            pl.BlockSpec((1, tm, d), lambda bi, i: (bi, i, 0)),
            pl.BlockSpec((1, 1, d), lambda bi, i: (bi, 0, 0)),
        ],
        out_specs=pl.BlockSpec((1, tm, d), lambda bi, i: (bi, i, 0)),
        compiler_params=_params(("arbitrary", "arbitrary")),
        name="merge",
    )(attn, rest, rest, rest, rest, rest, rest, rest, rest, rest, conv_w,
      w_pa, w_pc, w_o, x, g1)


FF_CHUNKS = ((0, 1536), (1536, 1280))


def _ffn_kernel(x_ref, g_ref, sh_ref, sc_ref, g2_ref, wg_ref, wu_ref, wd_ref, o_ref):
    ts = x_ref.shape[1] // N_SUB
    for si in range(N_SUB):
        rows = slice(si * ts, (si + 1) * ts)
        x = x_ref[0, rows, :]
        h = _norm_mod(x, g_ref[...], sh_ref[0], sc_ref[0]).astype(BF16)
        acc = None
        for c0, cw in FF_CHUNKS:
            gate = jnp.dot(h, wg_ref[:, c0:c0 + cw], preferred_element_type=F32)
            up = jnp.dot(h, wu_ref[:, c0:c0 + cw], preferred_element_type=F32)
            a = (gate * _sigmoid(gate) * up).astype(BF16)
            part = jnp.dot(a, wd_ref[c0:c0 + cw, :], preferred_element_type=F32)
            acc = part if acc is None else acc + part
        o_ref[0, rows, :] = x + g2_ref[0] * acc


def _ffn(x, g, shift, scale, g2, wg, wu, wd, tm):
    b, n, d = x.shape
    dff = wg.shape[1]
    assert sum(cw for _, cw in FF_CHUNKS) == dff
    vec = pl.BlockSpec((1, 1, d), lambda bi, i: (bi, 0, 0))
    return pl.pallas_call(
        _ffn_kernel,
        out_shape=jax.ShapeDtypeStruct((b, n, d), F32),
        grid=(b, n // tm),
        in_specs=[
            pl.BlockSpec((1, tm, d), lambda bi, i: (bi, i, 0)),
            pl.BlockSpec((1, d), lambda bi, i: (0, 0)),
            vec, vec, vec,
            _resident((d, dff)), _resident((d, dff)), _resident((dff, d)),
        ],
        out_specs=pl.BlockSpec((1, tm, d), lambda bi, i: (bi, i, 0)),
        compiler_params=_params(("arbitrary", "arbitrary")),
        name="ffn",
    )(x, g, shift, scale, g2, wg, wu, wd)


def _rope_tables_t(n):
    pos = jnp.arange(n, dtype=jnp.int32)
    row = (pos // GRID_W).astype(F32)
    col = (pos % GRID_W).astype(F32)
    pairs = HALF // 2
    inv = ROPE_BASE ** (-jnp.arange(pairs, dtype=F32) / pairs)
    ang = jnp.concatenate([inv[:, None] * row[None, :], inv[:, None] * col[None, :]], axis=0)
    return jnp.cos(ang), jnp.sin(ang)


def _tile(n, pref):
    return pref if n % pref == 0 else n


@jax.jit
def _forward(x, c, ctx, c_ctx, w_ada, b_ada, norm1_g, norm2_g, w_in, q_norm_g, k_norm_g,
             lambda_q1, lambda_k1, lambda_q2, lambda_k2, subln_g, conv_w, w_pa, w_pc, w_o,
             w_ffn_gate, w_ffn_up, w_ffn_down):
    b, n, d = x.shape
    nc = ctx.shape[1]
    depth = w_ada.shape[0]

    cmat = jnp.concatenate([c, c_ctx[None, :]], axis=0)
    cmat = jnp.pad(cmat, ((0, MOD_ROWS - (b + 1)), (0, 0)))
    mod_all = _adaln(cmat, w_ada, b_ada)

    cos_t, sin_t = _rope_tables_t(n)
    cos_c = jnp.ones((HALF, nc), F32)
    sin_c = jnp.zeros((HALF, nc), F32)

    tm = _tile(n, 512)
    tmc = _tile(nc, 512)
    tq = _tile(n, 512)
    tk = _tile(n, 512)
    reps = ATTN_W // QK_DIM

    for li in range(depth):
        last = li == depth - 1
        lam_init = 0.8 - 0.6 * math.exp(-0.3 * li)
        mods = [m.reshape(b, 1, d) for m in jnp.split(mod_all[li, :b], N_MOD, axis=-1)]
        cmods = [jnp.broadcast_to(m.reshape(1, 1, d), (b, 1, d))
                 for m in jnp.split(mod_all[li, b], N_MOD, axis=-1)]
        sh1, sc1, g1, sh2, sc2, g2 = mods
        csh1, csc1, cg1, csh2, csc2, cg2 = cmods

        w_att_t = w_in[li][:, :N_ATT].T.astype(BF16)
        w_rest = w_in[li][:, N_ATT:].astype(BF16)
        gcol = jnp.concatenate([jnp.tile(q_norm_g[li], reps) * (ATTN_SCALE * LOG2E),
                                jnp.tile(k_norm_g[li], reps)]).reshape(2 * ATTN_W, 1)
        lam_p = jnp.stack([lambda_q1[li], lambda_k1[li], lambda_q2[li], lambda_k2[li]])
        sub_g = subln_g[li].reshape(V_DIM, 1)
        n1 = norm1_g[li].reshape(1, d)
        n2 = norm2_g[li].reshape(1, d)
        wpa = w_pa[li].astype(BF16)
        wpc = w_pc[li].astype(BF16)
        wo = w_o[li].astype(BF16)
        wg = w_ffn_gate[li].astype(BF16)
        wu = w_ffn_up[li].astype(BF16)
        wd = w_ffn_down[li].astype(BF16)

        lat_t, rest = _proj(x, n1, sh1, sc1, w_att_t, gcol, cos_t, sin_t, w_rest, tm)
        ctx_t, rest_c = _proj(ctx, n1, csh1, csc1, w_att_t, gcol, cos_c, sin_c,
                              None if last else w_rest, tmc)

        attn = _attention(lat_t, ctx_t, lat_t, lam_p, sub_g, lam_init, tq, tk)
        x = _merge(attn, rest, conv_w[li], wpa, wpc, wo, x, g1, tm)
        x = _ffn(x, n2, sh2, sc2, g2, wg, wu, wd, tm)

        if not last:
            attn_c = _attention(ctx_t, ctx_t, None, lam_p, sub_g, lam_init, tmc, tk)
            ctx = _merge(attn_c, rest_c, conv_w[li], wpa, wpc, wo, ctx, cg1, tmc)
            ctx = _ffn(ctx, n2, csh2, csc2, cg2, wg, wu, wd, tmc)
    return x


def kernel(x, c, ctx, c_ctx, w_ada, b_ada, norm1_g, norm2_g, w_in, q_norm_g, k_norm_g,
           lambda_q1, lambda_k1, lambda_q2, lambda_k2, subln_g, conv_w, w_pa, w_pc, w_o,
           w_ffn_gate, w_ffn_up, w_ffn_down):
    return _forward(x, c, ctx, c_ctx, w_ada, b_ada, norm1_g, norm2_g, w_in, q_norm_g,
                    k_norm_g, lambda_q1, lambda_k1, lambda_q2, lambda_k2, subln_g, conv_w,
                    w_pa, w_pc, w_o, w_ffn_gate, w_ffn_up, w_ffn_down)
```

```python
import functools
import math

import jax
import jax.numpy as jnp
import numpy as np
from jax import lax
from jax.experimental import pallas as pl
from jax.experimental.pallas import tpu as pltpu

D_MODEL = 1024
N_HEADS = 8
QK_DIM = 64
HALF = QK_DIM // 2
V_DIM = 2 * QK_DIM
V_AUG = V_DIM + 16
ATTN_W = N_HEADS * V_DIM
N_ATT = 3 * ATTN_W
N_REST = 5 * D_MODEL
D_FF = 2816
GRID_W = 64
ROPE_BASE = 10000.0
ATTN_SCALE = QK_DIM ** -0.5
LOG2E = math.log2(math.e)
EPS = 1e-6
N_MOD = 6
MOD_ROWS = 8
HALO = 16
N_SUB = 2
TOKEN_TILE = 512
Q_TILE = 512
K_TILE = 512

BF16 = jnp.bfloat16
F32 = jnp.float32

VMEM_LIMIT = 56 * 1024 * 1024


def _params(sem):
    return pltpu.CompilerParams(dimension_semantics=sem, vmem_limit_bytes=VMEM_LIMIT)


def _sigmoid(x):
    return 1.0 / (1.0 + jnp.exp(-x))


def _norm_mod(x, g, shift, scale):
    ms = jnp.mean(x * x, axis=-1, keepdims=True)
    y = x * lax.rsqrt(ms + EPS) * g
    return y * (1.0 + scale) + shift


def _adaln_kernel(c_ref, w_ref, b_ref, o_ref):
    c = c_ref[...]
    s = (c * _sigmoid(c)).astype(BF16)
    w = w_ref[0].astype(BF16)
    o_ref[0] = jnp.dot(s, w, preferred_element_type=F32) + b_ref[0]


def _adaln(cmat, w_ada, b_ada):
    depth, d, n = w_ada.shape
    tn = 1536
    return pl.pallas_call(
        _adaln_kernel,
        out_shape=jax.ShapeDtypeStruct((depth, MOD_ROWS, n), F32),
        grid=(depth, n // tn),
        in_specs=[
            pl.BlockSpec((MOD_ROWS, d), lambda l, j: (0, 0)),
            pl.BlockSpec((1, d, tn), lambda l, j: (l, 0, j)),
            pl.BlockSpec((1, 1, tn), lambda l, j: (l, 0, j)),
        ],
        out_specs=pl.BlockSpec((1, MOD_ROWS, tn), lambda l, j: (l, 0, j)),
        compiler_params=_params(("arbitrary", "arbitrary")),
        name="adaln",
    )(cmat, w_ada, b_ada.reshape(depth, 1, n))


def _proj_kernel(*refs, with_rest):
    if with_rest:
        (x_ref, g_ref, sh_ref, sc_ref, wt_ref, gcol_ref, cos_ref, sin_ref, wr_ref,
         ot_ref, or_ref) = refs
    else:
        x_ref, g_ref, sh_ref, sc_ref, wt_ref, gcol_ref, cos_ref, sin_ref, ot_ref = refs
    c_all = cos_ref[...]
    s_all = sin_ref[...]
    tm = x_ref.shape[1]
    ts = tm // N_SUB
    for si in range(N_SUB):
        rows = slice(si * ts, (si + 1) * ts)
        h = _norm_mod(x_ref[0, rows, :], g_ref[...], sh_ref[0], sc_ref[0])
        if with_rest:
            hb = h.astype(BF16)
            for cb in range(N_REST // D_MODEL):
                cols = slice(cb * D_MODEL, (cb + 1) * D_MODEL)
                or_ref[0, rows, cols] = jnp.dot(
                    hb, wr_ref[:, cols], preferred_element_type=F32).astype(BF16)
        ht = h.T.astype(BF16)
        c = c_all[:, rows]
        s = s_all[:, rows]
        ot_ref[0, 2 * ATTN_W:, rows] = jnp.dot(
            wt_ref[2 * ATTN_W:, :], ht, preferred_element_type=F32).astype(BF16)
        for j in range(2):
            y = jnp.dot(wt_ref[j * ATTN_W:(j + 1) * ATTN_W, :], ht,
                        preferred_element_type=F32)
            for gi in range(ATTN_W // QK_DIM):
                r0 = gi * QK_DIM
                blk = y[r0:r0 + QK_DIM, :]
                ms = jnp.mean(blk * blk, axis=0, keepdims=True)
                g0 = j * ATTN_W + r0
                blk = blk * lax.rsqrt(ms + EPS) * gcol_ref[g0:g0 + QK_DIM, :]
                t1 = blk[:HALF, :]
                t2 = blk[HALF:, :]
                ot_ref[0, g0:g0 + HALF, rows] = (t1 * c - t2 * s).astype(BF16)
                ot_ref[0, g0 + HALF:g0 + QK_DIM, rows] = (t2 * c + t1 * s).astype(BF16)


def _resident(shape):
    return pl.BlockSpec(shape, lambda *_: (0,) * len(shape), pipeline_mode=pl.Buffered(1))


def _proj(x, g, shift, scale, w_t, gcol, cos_t, sin_t, w_rest, tm):
    b, n, d = x.shape
    with_rest = w_rest is not None
    in_specs = [
        pl.BlockSpec((1, tm, d), lambda bi, i: (bi, i, 0)),
        pl.BlockSpec((1, d), lambda bi, i: (0, 0)),
        pl.BlockSpec((1, 1, d), lambda bi, i: (bi, 0, 0)),
        pl.BlockSpec((1, 1, d), lambda bi, i: (bi, 0, 0)),
        _resident((N_ATT, d)),
        pl.BlockSpec((2 * ATTN_W, 1), lambda bi, i: (0, 0)),
        pl.BlockSpec((HALF, tm), lambda bi, i: (0, i)),
        pl.BlockSpec((HALF, tm), lambda bi, i: (0, i)),
    ]
    args = [x, g, shift, scale, w_t, gcol, cos_t, sin_t]
    out_shape = [jax.ShapeDtypeStruct((b, N_ATT, n), BF16)]
    out_specs = [pl.BlockSpec((1, N_ATT, tm), lambda bi, i: (bi, 0, i))]
    if with_rest:
        in_specs.append(_resident((d, N_REST)))
        args.append(w_rest)
        out_shape.append(jax.ShapeDtypeStruct((b, n, N_REST), BF16))
        out_specs.append(pl.BlockSpec((1, tm, N_REST), lambda bi, i: (bi, i, 0)))
    outs = pl.pallas_call(
        functools.partial(_proj_kernel, with_rest=with_rest),
        out_shape=out_shape,
        grid=(b, n // tm),
        in_specs=in_specs,
        out_specs=out_specs,
        compiler_params=_params(("arbitrary", "arbitrary")),
        name="proj" if with_rest else "proj_qkv",
    )(*args)
    return (outs[0], outs[1]) if with_rest else (outs[0], None)


def _attn_kernel(*refs, lam_init, n_lat_blocks):
    nb = n_lat_blocks
    if nb:
        (q_ref, qn_ref, kc_ref, vc_ref, k_ref, v_ref, lam_ref, g_ref, o_ref,
         kc_rows, vc_aug, k_rows, v_aug, qm_ref, sc_buf, mbc_buf, s_buf, mb_buf,
         m_ref, acc_ref) = refs
    else:
        (q_ref, kc_ref, vc_ref, lam_ref, g_ref, o_ref,
         kc_rows, vc_aug, qm_ref, sc_buf, mbc_buf, m_ref, acc_ref) = refs
    qi = pl.program_id(2)
    par = qi % 2 if nb else 0

    def build_qm(slot, src_ref):
        qt = src_ref[0]
        row = lax.broadcasted_iota(jnp.int32, qt.shape, 0)
        zero = jnp.zeros_like(qt)
        qm_ref[slot, 0] = jnp.where(row < QK_DIM, qt, zero)
        qm_ref[slot, 1] = jnp.where(row >= QK_DIM, qt, zero)

    def produce_ctx(qslot):
        kcb = kc_rows[...]
        for mi in range(2):
            s = jnp.dot(kcb, qm_ref[qslot, mi], preferred_element_type=F32)
            sc_buf[mi] = s
            mbc_buf[mi] = jnp.max(s, axis=0, keepdims=True)

    def produce(slot, jb, qslot, maps=(0, 1)):
        kb = k_rows[jb]
        for mi in maps:
            s = jnp.dot(kb, qm_ref[qslot, mi], preferred_element_type=F32)
            s_buf[slot, mi] = s
            mb_buf[slot, mi] = jnp.max(s, axis=0, keepdims=True)

    def consume_ctx(maps=(0, 1)):
        vcb = vc_aug[...]
        for mi in maps:
            mb = mbc_buf[mi]
            m_ref[mi] = mb
            acc_ref[mi] = jnp.dot(vcb, jnp.exp2(sc_buf[mi] - mb).astype(BF16),
                                  preferred_element_type=F32)

    def consume(slot, jb, maps=(0, 1)):
        vtb = v_aug[jb]
        for mi in maps:
            m_old = m_ref[mi]
            m_new = jnp.maximum(m_old, mb_buf[slot, mi])
            alpha = jnp.exp2(m_old - m_new)
            p = jnp.exp2(s_buf[slot, mi] - m_new).astype(BF16)
            m_ref[mi] = m_new
            acc_ref[mi] = alpha * acc_ref[mi] + jnp.dot(vtb, p, preferred_element_type=F32)

    @pl.when(qi == 0)
    def _():
        def ones_rows(n):
            r = lax.broadcasted_iota(jnp.int32, (V_AUG - V_DIM, n), 0)
            return jnp.where(r == 0, 1.0, 0.0).astype(BF16)

        kc_rows[...] = kc_ref[0].T
        vc_aug[:V_DIM, :] = vc_ref[0]
        vc_aug[V_DIM:, :] = ones_rows(vc_aug.shape[1])
        for jb in range(nb):
            tk = k_rows.shape[1]
            k_rows[jb] = k_ref[0, :, jb * tk:(jb + 1) * tk].T
            v_aug[jb, :V_DIM, :] = v_ref[0, :, jb * tk:(jb + 1) * tk]
            v_aug[jb, V_DIM:, :] = ones_rows(tk)
        build_qm(0, q_ref)
        produce_ctx(0)
        if nb:
            produce(2, 0, 0)

    if nb:
        build_qm(1 - par, qn_ref)
    else:
        consume_ctx()

    if nb:
        def slot_of(jb):
            return 2 if jb == 0 else jb % 2

        def emit(jb):
            for mi in range(2):
                if jb + 1 < nb:
                    produce(slot_of(jb + 1), jb + 1, par, (mi,))
                if jb == 0:
                    consume_ctx((mi,))
                    if mi == 1:
                        consume(slot_of(0), 0)
                else:
                    consume(slot_of(jb), jb, (mi,))
                if jb + 1 == nb:
                    if mi == 0:
                        produce_ctx(1 - par)
                    else:
                        produce(2, 0, 1 - par)

        n_pre = 2 if nb >= 4 else 0
        n_pairs = (nb - n_pre - 2) // 2
        for jb in range(n_pre):
            emit(jb)
        if n_pairs:
            def body(pi, carry):
                j0 = n_pre + 2 * pi
                for mi in range(2):
                    produce(1, j0 + 1, par, (mi,))
                    consume(0, j0, (mi,))
                for mi in range(2):
                    produce(0, j0 + 2, par, (mi,))
                    consume(1, j0 + 1, (mi,))
                return carry
            unroll = 3 if n_pairs % 3 == 0 else (2 if n_pairs % 2 == 0 else 1)
            lax.fori_loop(0, n_pairs, body, 0, unroll=unroll)
        for jb in range(n_pre + 2 * n_pairs, nb):
            emit(jb)

    lp = lam_ref[...]
    lam = (jnp.exp(jnp.sum(lp[0:1] * lp[1:2], axis=-1, keepdims=True))
           - jnp.exp(jnp.sum(lp[2:3] * lp[3:4], axis=-1, keepdims=True)) + lam_init)
    a1 = acc_ref[0]
    a2 = acc_ref[1]
    o = (a1[:V_DIM] / a1[V_DIM:V_DIM + 1]
         - lam * (a2[:V_DIM] / a2[V_DIM:V_DIM + 1]))
    ms = jnp.mean(o * o, axis=0, keepdims=True)
    o = o * lax.rsqrt(ms + EPS) * g_ref[...] * (1.0 - lam_init)
    o_ref[0] = o.T.astype(BF16)


def _attention(q_t, ctx_t, lat_t, lam_p, g_col, lam_init, tq, tk):
    b, _, nq = q_t.shape
    nc = ctx_t.shape[2]
    h = N_HEADS
    n_lat_blocks = 0 if lat_t is None else lat_t.shape[2] // tk
    n_q = nq // tq
    in_specs = [pl.BlockSpec((1, V_DIM, tq), lambda bi, hi, qi: (bi, hi, qi))]
    args = [q_t]
    if n_lat_blocks:
        in_specs.append(pl.BlockSpec(
            (1, V_DIM, tq), lambda bi, hi, qi: (bi, hi, jnp.minimum(qi + 1, n_q - 1))))
        args.append(q_t)
    in_specs += [
        pl.BlockSpec((1, V_DIM, nc), lambda bi, hi, qi: (bi, h + hi, 0)),
        pl.BlockSpec((1, V_DIM, nc), lambda bi, hi, qi: (bi, 2 * h + hi, 0)),
    ]
    args += [ctx_t, ctx_t]
    scratch = [pltpu.VMEM((nc, V_DIM), BF16), pltpu.VMEM((V_AUG, nc), BF16)]
    if n_lat_blocks:
        assert n_lat_blocks % 2 == 0, "latent key tiles are consumed in pairs"
        nl = lat_t.shape[2]
        in_specs += [
            pl.BlockSpec((1, V_DIM, nl), lambda bi, hi, qi: (bi, h + hi, 0)),
            pl.BlockSpec((1, V_DIM, nl), lambda bi, hi, qi: (bi, 2 * h + hi, 0)),
        ]
        args += [lat_t, lat_t]
        scratch += [pltpu.VMEM((n_lat_blocks, tk, V_DIM), BF16),
                    pltpu.VMEM((n_lat_blocks, V_AUG, tk), BF16)]
    in_specs += [
        pl.BlockSpec((4, QK_DIM), lambda bi, hi, qi: (0, 0)),
        pl.BlockSpec((V_DIM, 1), lambda bi, hi, qi: (0, 0)),
    ]
    args += [lam_p, g_col]
    scratch += [pltpu.VMEM((2, 2, V_DIM, tq), BF16),
                pltpu.VMEM((2, nc, tq), F32), pltpu.VMEM((2, 1, tq), F32)]
    if n_lat_blocks:
        scratch += [pltpu.VMEM((3, 2, tk, tq), F32), pltpu.VMEM((3, 2, 1, tq), F32)]
    scratch += [pltpu.VMEM((2, 1, tq), F32), pltpu.VMEM((2, V_AUG, tq), F32)]
    return pl.pallas_call(
        functools.partial(_attn_kernel, lam_init=lam_init, n_lat_blocks=n_lat_blocks),
        out_shape=jax.ShapeDtypeStruct((b, nq, ATTN_W), BF16),
        grid=(b, h, n_q),
        in_specs=in_specs,
        out_specs=pl.BlockSpec((1, tq, V_DIM), lambda bi, hi, qi: (bi, qi, hi)),
        scratch_shapes=scratch,
        compiler_params=_params(("arbitrary", "arbitrary", "arbitrary")),
        name="diff_attn" if n_lat_blocks else "diff_attn_ctx",
    )(*args)


def _merge_kernel(attn_ref, cx_ref, cb_ref, cc_ref, ga_ref, gc_ref,
                  cxp_ref, ccp_ref, cxn_ref, ccn_ref, cw_ref,
                  wpa_ref, wpc_ref, wo_ref, x_ref, g1_ref, o_ref):
    i = pl.program_id(1)
    last = pl.num_programs(1) - 1
    tm = x_ref.shape[1]
    u = cc_ref[0].astype(F32) * cx_ref[0].astype(F32)
    prev = (ccp_ref[0, HALO - 1:HALO, :].astype(F32) * cxp_ref[0, HALO - 1:HALO, :].astype(F32))
    nxt = ccn_ref[0, 0:1, :].astype(F32) * cxn_ref[0, 0:1, :].astype(F32)
    prev = jnp.where(i == 0, 0.0, prev)
    nxt = jnp.where(i == last, 0.0, nxt)
    rid = lax.broadcasted_iota(jnp.int32, u.shape, 0)
    u_prev = jnp.where(rid == 0, prev, pltpu.roll(u, 1, 0))
    u_next = jnp.where(rid == tm - 1, nxt, pltpu.roll(u, tm - 1, 0))
    cw = cw_ref[...]
    ts = tm // N_SUB
    subs = [slice(si * ts, (si + 1) * ts) for si in range(N_SUB)]
    ya = [jnp.dot(attn_ref[0, r, :], wpa_ref[...], preferred_element_type=F32) for r in subs]
    conv = u_prev * cw[0:1] + u * cw[1:2] + u_next * cw[2:3]
    y_conv = (cb_ref[0].astype(F32) * conv).astype(BF16)
    yc = [jnp.dot(y_conv[r], wpc_ref[...], preferred_element_type=F32) for r in subs]
    for si, r in enumerate(subs):
        z = (_sigmoid(ga_ref[0, r, :].astype(F32)) * ya[si]
             + _sigmoid(gc_ref[0, r, :].astype(F32)) * yc[si])
        mix = jnp.dot(z.astype(BF16), wo_ref[...], preferred_element_type=F32)
        o_ref[0, r, :] = x_ref[0, r, :] + g1_ref[0] * mix


def _merge(attn, rest, conv_w, w_pa, w_pc, w_o, x, g1, tm):
    b, n, d = x.shape
    nh = n // HALO
    th = tm // HALO
    col = lambda c: pl.BlockSpec((1, tm, d), lambda bi, i, c=c: (bi, i, c))
    prev = lambda c: pl.BlockSpec(
        (1, HALO, d), lambda bi, i, c=c: (bi, jnp.maximum(i * th - 1, 0), c))
    nxt = lambda c: pl.BlockSpec(
        (1, HALO, d), lambda bi, i, c=c: (bi, jnp.minimum((i + 1) * th, nh - 1), c))
    return pl.pallas_call(
        _merge_kernel,
        out_shape=jax.ShapeDtypeStruct((b, n, d), F32),
        grid=(b, n // tm),
        in_specs=[
            pl.BlockSpec((1, tm, d), lambda bi, i: (bi, i, 0)),
            col(0), col(1), col(2), col(3), col(4),
            prev(0), prev(2), nxt(0), nxt(2),
            pl.BlockSpec((3, d), lambda bi, i: (0, 0)),
            _resident((d, d)), _resident((d, d)), _resident((d, d)),
            pl.BlockSpec((1, tm, d), lambda bi, i: (bi, i, 0)),
            pl.BlockSpec((1, 1, d), lambda bi, i: (bi, 0, 0)),
        ],
        out_specs=pl.BlockSpec((1, tm, d), lambda bi, i: (bi, i, 0)),
        compiler_params=_params(("arbitrary", "arbitrary")),
        name="merge",
    )(attn, rest, rest, rest, rest, rest, rest, rest, rest, rest, conv_w,
      w_pa, w_pc, w_o, x, g1)


FF_CHUNKS = ((0, 1536), (1536, 1280))


def _ffn_kernel(x_ref, g_ref, sh_ref, sc_ref, g2_ref, wg_ref, wu_ref, wd_ref, o_ref):
    ts = x_ref.shape[1] // N_SUB
    for si in range(N_SUB):
        rows = slice(si * ts, (si + 1) * ts)
        x = x_ref[0, rows, :]
        h = _norm_mod(x, g_ref[...], sh_ref[0], sc_ref[0]).astype(BF16)
        acc = None
        for c0, cw in FF_CHUNKS:
            gate = jnp.dot(h, wg_ref[:, c0:c0 + cw], preferred_element_type=F32)
            up = jnp.dot(h, wu_ref[:, c0:c0 + cw], preferred_element_type=F32)
            a = (gate * _sigmoid(gate) * up).astype(BF16)
            part = jnp.dot(a, wd_ref[c0:c0 + cw, :], preferred_element_type=F32)
            acc = part if acc is None else acc + part
        o_ref[0, rows, :] = x + g2_ref[0] * acc


def _ffn(x, g, shift, scale, g2, wg, wu, wd, tm):
    b, n, d = x.shape
    dff = wg.shape[1]
    assert sum(cw for _, cw in FF_CHUNKS) == dff
    vec = pl.BlockSpec((1, 1, d), lambda bi, i: (bi, 0, 0))
    return pl.pallas_call(
        _ffn_kernel,
        out_shape=jax.ShapeDtypeStruct((b, n, d), F32),
        grid=(b, n // tm),
        in_specs=[
            pl.BlockSpec((1, tm, d), lambda bi, i: (bi, i, 0)),
            pl.BlockSpec((1, d), lambda bi, i: (0, 0)),
            vec, vec, vec,
            _resident((d, dff)), _resident((d, dff)), _resident((dff, d)),
        ],
        out_specs=pl.BlockSpec((1, tm, d), lambda bi, i: (bi, i, 0)),
        compiler_params=_params(("arbitrary", "arbitrary")),
        name="ffn",
    )(x, g, shift, scale, g2, wg, wu, wd)


def _rope_tables_t(n):
    pos = np.arange(n)
    row = (pos // GRID_W).astype(np.float64)
    col = (pos % GRID_W).astype(np.float64)
    pairs = HALF // 2
    inv = ROPE_BASE ** (-np.arange(pairs, dtype=np.float64) / pairs)
    ang = np.concatenate([inv[:, None] * row[None, :], inv[:, None] * col[None, :]], axis=0)
    return jnp.asarray(np.cos(ang), F32), jnp.asarray(np.sin(ang), F32)


def _tile(n, pref):
    return pref if n % pref == 0 else n


@jax.jit
def _forward(x, c, ctx, c_ctx, w_ada, b_ada, norm1_g, norm2_g, w_in, q_norm_g, k_norm_g,
             lambda_q1, lambda_k1, lambda_q2, lambda_k2, subln_g, conv_w, w_pa, w_pc, w_o,
             w_ffn_gate, w_ffn_up, w_ffn_down):
    b, n, d = x.shape
    nc = ctx.shape[1]
    depth = w_ada.shape[0]

    cmat = jnp.concatenate([c, c_ctx[None, :]], axis=0)
    cmat = jnp.pad(cmat, ((0, MOD_ROWS - (b + 1)), (0, 0)))
    mod_all = _adaln(cmat, w_ada, b_ada)

    cos_t, sin_t = _rope_tables_t(n)
    cos_c = jnp.ones((HALF, nc), F32)
    sin_c = jnp.zeros((HALF, nc), F32)

    tm = _tile(n, TOKEN_TILE)
    tmc = _tile(nc, TOKEN_TILE)
    tq = _tile(n, Q_TILE)
    tk = _tile(n, K_TILE)
    reps = ATTN_W // QK_DIM

    for li in range(depth):
        last = li == depth - 1
        lam_init = 0.8 - 0.6 * math.exp(-0.3 * li)
        mods = [m.reshape(b, 1, d) for m in jnp.split(mod_all[li, :b], N_MOD, axis=-1)]
        cmods = [jnp.broadcast_to(m.reshape(1, 1, d), (b, 1, d))
                 for m in jnp.split(mod_all[li, b], N_MOD, axis=-1)]
        sh1, sc1, g1, sh2, sc2, g2 = mods
        csh1, csc1, cg1, csh2, csc2, cg2 = cmods

        w_att_t = w_in[li][:, :N_ATT].T.astype(BF16)
        w_rest = w_in[li][:, N_ATT:].astype(BF16)
        gcol = jnp.concatenate([jnp.tile(q_norm_g[li], reps) * (ATTN_SCALE * LOG2E),
                                jnp.tile(k_norm_g[li], reps)]).reshape(2 * ATTN_W, 1)
        lam_p = jnp.stack([lambda_q1[li], lambda_k1[li], lambda_q2[li], lambda_k2[li]])
        sub_g = subln_g[li].reshape(V_DIM, 1)
        n1 = norm1_g[li].reshape(1, d)
        n2 = norm2_g[li].reshape(1, d)
        wpa = w_pa[li].astype(BF16)
        wpc = w_pc[li].astype(BF16)
        wo = w_o[li].astype(BF16)
        wg = w_ffn_gate[li].astype(BF16)
        wu = w_ffn_up[li].astype(BF16)
        wd = w_ffn_down[li].astype(BF16)

        lat_t, rest = _proj(x, n1, sh1, sc1, w_att_t, gcol, cos_t, sin_t, w_rest, tm)
        ctx_t, rest_c = _proj(ctx, n1, csh1, csc1, w_att_t, gcol, cos_c, sin_c,
                              None if last else w_rest, tmc)

        attn = _attention(lat_t, ctx_t, lat_t, lam_p, sub_g, lam_init, tq, tk)
        x = _merge(attn, rest, conv_w[li], wpa, wpc, wo, x, g1, tm)
        x = _ffn(x, n2, sh2, sc2, g2, wg, wu, wd, tm)

        if not last:
            attn_c = _attention(ctx_t, ctx_t, None, lam_p, sub_g, lam_init, tmc, tk)
            ctx = _merge(attn_c, rest_c, conv_w[li], wpa, wpc, wo, ctx, cg1, tmc)
            ctx = _ffn(ctx, n2, csh2, csc2, cg2, wg, wu, wd, tmc)
    return x


def kernel(x, c, ctx, c_ctx, w_ada, b_ada, norm1_g, norm2_g, w_in, q_norm_g, k_norm_g,
           lambda_q1, lambda_k1, lambda_q2, lambda_k2, subln_g, conv_w, w_pa, w_pc, w_o,
           w_ffn_gate, w_ffn_up, w_ffn_down):
    return _forward(x, c, ctx, c_ctx, w_ada, b_ada, norm1_g, norm2_g, w_in, q_norm_g,
                    k_norm_g, lambda_q1, lambda_k1, lambda_q2, lambda_k2, subln_g, conv_w,
                    w_pa, w_pc, w_o, w_ffn_gate, w_ffn_up, w_ffn_down)
```

```python
import functools
import math

import jax
import jax.numpy as jnp
import numpy as np
from jax import lax
from jax.experimental import pallas as pl
from jax.experimental.pallas import tpu as pltpu

D_MODEL = 1024
N_HEADS = 8
QK_DIM = 64
HALF = QK_DIM // 2
V_DIM = 2 * QK_DIM
V_AUG = V_DIM + 16
ATTN_W = N_HEADS * V_DIM
N_ATT = 3 * ATTN_W
N_REST = 5 * D_MODEL
D_FF = 2816
GRID_W = 64
ROPE_BASE = 10000.0
ATTN_SCALE = QK_DIM ** -0.5
LOG2E = math.log2(math.e)
EPS = 1e-6
N_MOD = 6
MOD_ROWS = 8
HALO = 16
N_SUB = 2
TOKEN_TILE = 512
Q_TILE = 512
K_TILE = 512

BF16 = jnp.bfloat16
F32 = jnp.float32

VMEM_LIMIT = 56 * 1024 * 1024


def _params(sem):
    return pltpu.CompilerParams(dimension_semantics=sem, vmem_limit_bytes=VMEM_LIMIT)


def _sigmoid(x):
    return 1.0 / (1.0 + jnp.exp(-x))


def _norm_mod(x, g, shift, scale):
    ms = jnp.mean(x * x, axis=-1, keepdims=True)
    y = x * lax.rsqrt(ms + EPS) * g
    return y * (1.0 + scale) + shift


def _adaln_kernel(c_ref, w_ref, b_ref, o_ref):
    c = c_ref[...]
    s = (c * _sigmoid(c)).astype(BF16)
    w = w_ref[0].astype(BF16)
    o_ref[0] = jnp.dot(s, w, preferred_element_type=F32) + b_ref[0]


def _adaln(cmat, w_ada, b_ada):
    depth, d, n = w_ada.shape
    tn = 1536
    return pl.pallas_call(
        _adaln_kernel,
        out_shape=jax.ShapeDtypeStruct((depth, MOD_ROWS, n), F32),
        grid=(depth, n // tn),
        in_specs=[
            pl.BlockSpec((MOD_ROWS, d), lambda l, j: (0, 0)),
            pl.BlockSpec((1, d, tn), lambda l, j: (l, 0, j)),
            pl.BlockSpec((1, 1, tn), lambda l, j: (l, 0, j)),
        ],
        out_specs=pl.BlockSpec((1, MOD_ROWS, tn), lambda l, j: (l, 0, j)),
        compiler_params=_params(("arbitrary", "arbitrary")),
        name="adaln",
    )(cmat, w_ada, b_ada.reshape(depth, 1, n))


def _proj_kernel(*refs, with_rest):
    if with_rest:
        (x_ref, g_ref, sh_ref, sc_ref, wt_ref, gcol_ref, cos_ref, sin_ref, wr_ref,
         ot_ref, or_ref) = refs
    else:
        x_ref, g_ref, sh_ref, sc_ref, wt_ref, gcol_ref, cos_ref, sin_ref, ot_ref = refs
    c_all = cos_ref[...]
    s_all = sin_ref[...]
    tm = x_ref.shape[1]
    ts = tm // N_SUB
    for si in range(N_SUB):
        rows = slice(si * ts, (si + 1) * ts)
        h = _norm_mod(x_ref[0, rows, :], g_ref[...], sh_ref[0], sc_ref[0])
        hb = h.astype(BF16)

        def rest_cols(lo, hi):
            for cb in range(lo, hi):
                cols = slice(cb * D_MODEL, (cb + 1) * D_MODEL)
                or_ref[0, rows, cols] = jnp.dot(
                    hb, wr_ref[:, cols], preferred_element_type=F32).astype(BF16)

        n_rest = N_REST // D_MODEL if with_rest else 0
        rest_cols(0, min(2, n_rest))
        ht = h.T.astype(BF16)
        c = c_all[:, rows]
        s = s_all[:, rows]
        ot_ref[0, 2 * ATTN_W:, rows] = jnp.dot(
            wt_ref[2 * ATTN_W:, :], ht, preferred_element_type=F32).astype(BF16)
        for j in range(2):
            y = jnp.dot(wt_ref[j * ATTN_W:(j + 1) * ATTN_W, :], ht,
                        preferred_element_type=F32)
            for gi in range(ATTN_W // QK_DIM):
                r0 = gi * QK_DIM
                blk = y[r0:r0 + QK_DIM, :]
                ms = jnp.mean(blk * blk, axis=0, keepdims=True)
                g0 = j * ATTN_W + r0
                blk = blk * lax.rsqrt(ms + EPS) * gcol_ref[g0:g0 + QK_DIM, :]
                t1 = blk[:HALF, :]
                t2 = blk[HALF:, :]
                ot_ref[0, g0:g0 + HALF, rows] = (t1 * c - t2 * s).astype(BF16)
                ot_ref[0, g0 + HALF:g0 + QK_DIM, rows] = (t2 * c + t1 * s).astype(BF16)
        rest_cols(2, n_rest)


def _resident(shape):
    return pl.BlockSpec(shape, lambda *_: (0,) * len(shape), pipeline_mode=pl.Buffered(1))


def _proj(x, g, shift, scale, w_t, gcol, cos_t, sin_t, w_rest, tm):
    b, n, d = x.shape
    with_rest = w_rest is not None
    in_specs = [
        pl.BlockSpec((1, tm, d), lambda bi, i: (bi, i, 0)),
        pl.BlockSpec((1, d), lambda bi, i: (0, 0)),
        pl.BlockSpec((1, 1, d), lambda bi, i: (bi, 0, 0)),
        pl.BlockSpec((1, 1, d), lambda bi, i: (bi, 0, 0)),
        _resident((N_ATT, d)),
        pl.BlockSpec((2 * ATTN_W, 1), lambda bi, i: (0, 0)),
        pl.BlockSpec((HALF, tm), lambda bi, i: (0, i)),
        pl.BlockSpec((HALF, tm), lambda bi, i: (0, i)),
    ]
    args = [x, g, shift, scale, w_t, gcol, cos_t, sin_t]
    out_shape = [jax.ShapeDtypeStruct((b, N_ATT, n), BF16)]
    out_specs = [pl.BlockSpec((1, N_ATT, tm), lambda bi, i: (bi, 0, i))]
    if with_rest:
        in_specs.append(_resident((d, N_REST)))
        args.append(w_rest)
        out_shape.append(jax.ShapeDtypeStruct((b, n, N_REST), BF16))
        out_specs.append(pl.BlockSpec((1, tm, N_REST), lambda bi, i: (bi, i, 0)))
    outs = pl.pallas_call(
        functools.partial(_proj_kernel, with_rest=with_rest),
        out_shape=out_shape,
        grid=(b, n // tm),
        in_specs=in_specs,
        out_specs=out_specs,
        compiler_params=_params(("arbitrary", "arbitrary")),
        name="proj" if with_rest else "proj_qkv",
    )(*args)
    return (outs[0], outs[1]) if with_rest else (outs[0], None)


def _attn_kernel(*refs, lam_init, n_lat_blocks):
    nb = n_lat_blocks
    if nb:
        (q_ref, qn_ref, kc_ref, vc_ref, k_ref, v_ref, lam_ref, g_ref, o_ref,
         kc_rows, vc_aug, k_rows, v_aug, qm_ref, sc_buf, mbc_buf, s_buf, mb_buf,
         m_ref, acc_ref) = refs
    else:
        (q_ref, kc_ref, vc_ref, lam_ref, g_ref, o_ref,
         kc_rows, vc_aug, qm_ref, sc_buf, mbc_buf, m_ref, acc_ref) = refs
    qi = pl.program_id(2)
    par = qi % 2 if nb else 0

    def build_qm(slot, src_ref):
        qt = src_ref[0]
        row = lax.broadcasted_iota(jnp.int32, qt.shape, 0)
        zero = jnp.zeros_like(qt)
        qm_ref[slot, 0] = jnp.where(row < QK_DIM, qt, zero)
        qm_ref[slot, 1] = jnp.where(row >= QK_DIM, qt, zero)

    def produce_ctx(qslot):
        kcb = kc_rows[...]
        for mi in range(2):
            s = jnp.dot(kcb, qm_ref[qslot, mi], preferred_element_type=F32)
            sc_buf[mi] = s
            mbc_buf[mi] = jnp.max(s, axis=0, keepdims=True)

    def produce(slot, jb, qslot, maps=(0, 1)):
        kb = k_rows[jb]
        for mi in maps:
            s = jnp.dot(kb, qm_ref[qslot, mi], preferred_element_type=F32)
            s_buf[slot, mi] = s
            mb_buf[slot, mi] = jnp.max(s, axis=0, keepdims=True)

    def consume_ctx(maps=(0, 1)):
        vcb = vc_aug[...]
        for mi in maps:
            mb = mbc_buf[mi]
            m_ref[mi] = mb
            acc_ref[mi] = jnp.dot(vcb, jnp.exp2(sc_buf[mi] - mb).astype(BF16),
                                  preferred_element_type=F32)

    def consume(slot, jb, maps=(0, 1)):
        vtb = v_aug[jb]
        for mi in maps:
            m_old = m_ref[mi]
            m_new = jnp.maximum(m_old, mb_buf[slot, mi])
            alpha = jnp.exp2(m_old - m_new)
            p = jnp.exp2(s_buf[slot, mi] - m_new).astype(BF16)
            m_ref[mi] = m_new
            acc_ref[mi] = alpha * acc_ref[mi] + jnp.dot(vtb, p, preferred_element_type=F32)

    @pl.when(qi == 0)
    def _():
        def ones_rows(n):
            r = lax.broadcasted_iota(jnp.int32, (V_AUG - V_DIM, n), 0)
            return jnp.where(r == 0, 1.0, 0.0).astype(BF16)

        kc_rows[...] = kc_ref[0].T
        vc_aug[:V_DIM, :] = vc_ref[0]
        vc_aug[V_DIM:, :] = ones_rows(vc_aug.shape[1])
        for jb in range(nb):
            tk = k_rows.shape[1]
            k_rows[jb] = k_ref[0, :, jb * tk:(jb + 1) * tk].T
            v_aug[jb, :V_DIM, :] = v_ref[0, :, jb * tk:(jb + 1) * tk]
            v_aug[jb, V_DIM:, :] = ones_rows(tk)
        build_qm(0, q_ref)
        produce_ctx(0)
        if nb:
            produce(2, 0, 0)

    if nb:
        build_qm(1 - par, qn_ref)
    else:
        consume_ctx()

    if nb:
        def slot_of(jb):
            return 2 if jb == 0 else jb % 2

        def emit(jb):
            for mi in range(2):
                if jb + 1 < nb:
                    produce(slot_of(jb + 1), jb + 1, par, (mi,))
                if jb == 0:
                    consume_ctx((mi,))
                    if mi == 1:
                        consume(slot_of(0), 0)
                else:
                    consume(slot_of(jb), jb, (mi,))
                if jb + 1 == nb:
                    if mi == 0:
                        produce_ctx(1 - par)
                    else:
                        produce(2, 0, 1 - par)

        n_pre = 2 if nb >= 4 else 0
        n_pairs = (nb - n_pre - 2) // 2
        for jb in range(n_pre):
            emit(jb)
        if n_pairs:
            def body(pi, carry):
                j0 = n_pre + 2 * pi
                for mi in range(2):
                    produce(1, j0 + 1, par, (mi,))
                    consume(0, j0, (mi,))
                for mi in range(2):
                    produce(0, j0 + 2, par, (mi,))
                    consume(1, j0 + 1, (mi,))
                return carry
            unroll = 3 if n_pairs % 3 == 0 else (2 if n_pairs % 2 == 0 else 1)
            lax.fori_loop(0, n_pairs, body, 0, unroll=unroll)
        for jb in range(n_pre + 2 * n_pairs, nb):
            emit(jb)

    lp = lam_ref[...]
    lam = (jnp.exp(jnp.sum(lp[0:1] * lp[1:2], axis=-1, keepdims=True))
           - jnp.exp(jnp.sum(lp[2:3] * lp[3:4], axis=-1, keepdims=True)) + lam_init)
    a1 = acc_ref[0]
    a2 = acc_ref[1]
    o = (a1[:V_DIM] / a1[V_DIM:V_DIM + 1]
         - lam * (a2[:V_DIM] / a2[V_DIM:V_DIM + 1]))
    ms = jnp.mean(o * o, axis=0, keepdims=True)
    o = o * lax.rsqrt(ms + EPS) * g_ref[...] * (1.0 - lam_init)
    o_ref[0] = o.T.astype(BF16)


def _attention(q_t, ctx_t, lat_t, lam_p, g_col, lam_init, tq, tk):
    b, _, nq = q_t.shape
    nc = ctx_t.shape[2]
    h = N_HEADS
    n_lat_blocks = 0 if lat_t is None else lat_t.shape[2] // tk
    n_q = nq // tq
    in_specs = [pl.BlockSpec((1, V_DIM, tq), lambda bi, hi, qi: (bi, hi, qi))]
    args = [q_t]
    if n_lat_blocks:
        in_specs.append(pl.BlockSpec(
            (1, V_DIM, tq), lambda bi, hi, qi: (bi, hi, jnp.minimum(qi + 1, n_q - 1))))
        args.append(q_t)
    in_specs += [
        pl.BlockSpec((1, V_DIM, nc), lambda bi, hi, qi: (bi, h + hi, 0)),
        pl.BlockSpec((1, V_DIM, nc), lambda bi, hi, qi: (bi, 2 * h + hi, 0)),
    ]
    args += [ctx_t, ctx_t]
    scratch = [pltpu.VMEM((nc, V_DIM), BF16), pltpu.VMEM((V_AUG, nc), BF16)]
    if n_lat_blocks:
        assert n_lat_blocks % 2 == 0, "latent key tiles are consumed in pairs"
        nl = lat_t.shape[2]
        in_specs += [
            pl.BlockSpec((1, V_DIM, nl), lambda bi, hi, qi: (bi, h + hi, 0)),
            pl.BlockSpec((1, V_DIM, nl), lambda bi, hi, qi: (bi, 2 * h + hi, 0)),
        ]
        args += [lat_t, lat_t]
        scratch += [pltpu.VMEM((n_lat_blocks, tk, V_DIM), BF16),
                    pltpu.VMEM((n_lat_blocks, V_AUG, tk), BF16)]
    in_specs += [
        pl.BlockSpec((4, QK_DIM), lambda bi, hi, qi: (0, 0)),
        pl.BlockSpec((V_DIM, 1), lambda bi, hi, qi: (0, 0)),
    ]
    args += [lam_p, g_col]
    scratch += [pltpu.VMEM((2, 2, V_DIM, tq), BF16),
                pltpu.VMEM((2, nc, tq), F32), pltpu.VMEM((2, 1, tq), F32)]
    if n_lat_blocks:
        scratch += [pltpu.VMEM((3, 2, tk, tq), F32), pltpu.VMEM((3, 2, 1, tq), F32)]
    scratch += [pltpu.VMEM((2, 1, tq), F32), pltpu.VMEM((2, V_AUG, tq), F32)]
    return pl.pallas_call(
        functools.partial(_attn_kernel, lam_init=lam_init, n_lat_blocks=n_lat_blocks),
        out_shape=jax.ShapeDtypeStruct((b, nq, ATTN_W), BF16),
        grid=(b, h, n_q),
        in_specs=in_specs,
        out_specs=pl.BlockSpec((1, tq, V_DIM), lambda bi, hi, qi: (bi, qi, hi)),
        scratch_shapes=scratch,
        compiler_params=_params(("arbitrary", "arbitrary", "arbitrary")),
        name="diff_attn" if n_lat_blocks else "diff_attn_ctx",
    )(*args)


def _merge_kernel(attn_ref, cx_ref, cb_ref, cc_ref, ga_ref, gc_ref,
                  cxp_ref, ccp_ref, cxn_ref, ccn_ref, cw_ref,
                  wpa_ref, wpc_ref, wo_ref, x_ref, g1_ref, o_ref):
    i = pl.program_id(1)
    last = pl.num_programs(1) - 1
    tm = x_ref.shape[1]
    u = cc_ref[0].astype(F32) * cx_ref[0].astype(F32)
    prev = (ccp_ref[0, HALO - 1:HALO, :].astype(F32) * cxp_ref[0, HALO - 1:HALO, :].astype(F32))
    nxt = ccn_ref[0, 0:1, :].astype(F32) * cxn_ref[0, 0:1, :].astype(F32)
    prev = jnp.where(i == 0, 0.0, prev)
    nxt = jnp.where(i == last, 0.0, nxt)
    rid = lax.broadcasted_iota(jnp.int32, u.shape, 0)
    u_prev = jnp.where(rid == 0, prev, pltpu.roll(u, 1, 0))
    u_next = jnp.where(rid == tm - 1, nxt, pltpu.roll(u, tm - 1, 0))
    cw = cw_ref[...]
    ts = tm // N_SUB
    subs = [slice(si * ts, (si + 1) * ts) for si in range(N_SUB)]
    ya = [jnp.dot(attn_ref[0, r, :], wpa_ref[...], preferred_element_type=F32) for r in subs]
    conv = u_prev * cw[0:1] + u * cw[1:2] + u_next * cw[2:3]
    y_conv = (cb_ref[0].astype(F32) * conv).astype(BF16)
    yc = [jnp.dot(y_conv[r], wpc_ref[...], preferred_element_type=F32) for r in subs]
    for si, r in enumerate(subs):
        z = (_sigmoid(ga_ref[0, r, :].astype(F32)) * ya[si]
             + _sigmoid(gc_ref[0, r, :].astype(F32)) * yc[si])
        mix = jnp.dot(z.astype(BF16), wo_ref[...], preferred_element_type=F32)
        o_ref[0, r, :] = x_ref[0, r, :] + g1_ref[0] * mix


def _merge(attn, rest, conv_w, w_pa, w_pc, w_o, x, g1, tm):
    b, n, d = x.shape
    nh = n // HALO
    th = tm // HALO
    col = lambda c: pl.BlockSpec((1, tm, d), lambda bi, i, c=c: (bi, i, c))
    prev = lambda c: pl.BlockSpec(
        (1, HALO, d), lambda bi, i, c=c: (bi, jnp.maximum(i * th - 1, 0), c))
    nxt = lambda c: pl.BlockSpec(
        (1, HALO, d), lambda bi, i, c=c: (bi, jnp.minimum((i + 1) * th, nh - 1), c))
    return pl.pallas_call(
        _merge_kernel,
        out_shape=jax.ShapeDtypeStruct((b, n, d), F32),
        grid=(b, n // tm),
        in_specs=[
            pl.BlockSpec((1, tm, d), lambda bi, i: (bi, i, 0)),
            col(0), col(1), col(2), col(3), col(4),
            prev(0), prev(2), nxt(0), nxt(2),
            pl.BlockSpec((3, d), lambda bi, i: (0, 0)),
            _resident((d, d)), _resident((d, d)), _resident((d, d)),
            pl.BlockSpec((1, tm, d), lambda bi, i: (bi, i, 0)),
            pl.BlockSpec((1, 1, d), lambda bi, i: (bi, 0, 0)),
        ],
        out_specs=pl.BlockSpec((1, tm, d), lambda bi, i: (bi, i, 0)),
        compiler_params=_params(("arbitrary", "arbitrary")),
        name="merge",
    )(attn, rest, rest, rest, rest, rest, rest, rest, rest, rest, conv_w,
      w_pa, w_pc, w_o, x, g1)


FF_CHUNKS = ((0, 1536), (1536, 1280))


def _ffn_kernel(x_ref, g_ref, sh_ref, sc_ref, g2_ref, wg_ref, wu_ref, wd_ref, o_ref):
    ts = x_ref.shape[1] // N_SUB
    for si in range(N_SUB):
        rows = slice(si * ts, (si + 1) * ts)
        x = x_ref[0, rows, :]
        h = _norm_mod(x, g_ref[...], sh_ref[0], sc_ref[0]).astype(BF16)
        acc = None
        for c0, cw in FF_CHUNKS:
            gate = jnp.dot(h, wg_ref[:, c0:c0 + cw], preferred_element_type=F32)
            up = jnp.dot(h, wu_ref[:, c0:c0 + cw], preferred_element_type=F32)
            a = (gate * _sigmoid(gate) * up).astype(BF16)
            part = jnp.dot(a, wd_ref[c0:c0 + cw, :], preferred_element_type=F32)
            acc = part if acc is None else acc + part
        o_ref[0, rows, :] = x + g2_ref[0] * acc


def _ffn(x, g, shift, scale, g2, wg, wu, wd, tm):
    b, n, d = x.shape
    dff = wg.shape[1]
    assert sum(cw for _, cw in FF_CHUNKS) == dff
    vec = pl.BlockSpec((1, 1, d), lambda bi, i: (bi, 0, 0))
    return pl.pallas_call(
        _ffn_kernel,
        out_shape=jax.ShapeDtypeStruct((b, n, d), F32),
        grid=(b, n // tm),
        in_specs=[
            pl.BlockSpec((1, tm, d), lambda bi, i: (bi, i, 0)),
            pl.BlockSpec((1, d), lambda bi, i: (0, 0)),
            vec, vec, vec,
            _resident((d, dff)), _resident((d, dff)), _resident((dff, d)),
        ],
        out_specs=pl.BlockSpec((1, tm, d), lambda bi, i: (bi, i, 0)),
        compiler_params=_params(("arbitrary", "arbitrary")),
        name="ffn",
    )(x, g, shift, scale, g2, wg, wu, wd)


def _rope_tables_t(n):
    pos = np.arange(n)
    row = (pos // GRID_W).astype(np.float64)
    col = (pos % GRID_W).astype(np.float64)
    pairs = HALF // 2
    inv = ROPE_BASE ** (-np.arange(pairs, dtype=np.float64) / pairs)
    ang = np.concatenate([inv[:, None] * row[None, :], inv[:, None] * col[None, :]], axis=0)
    return jnp.asarray(np.cos(ang), F32), jnp.asarray(np.sin(ang), F32)


def _tile(n, pref):
    return pref if n % pref == 0 else n


@jax.jit
def _forward(x, c, ctx, c_ctx, w_ada, b_ada, norm1_g, norm2_g, w_in, q_norm_g, k_norm_g,
             lambda_q1, lambda_k1, lambda_q2, lambda_k2, subln_g, conv_w, w_pa, w_pc, w_o,
             w_ffn_gate, w_ffn_up, w_ffn_down):
    b, n, d = x.shape
    nc = ctx.shape[1]
    depth = w_ada.shape[0]

    cmat = jnp.concatenate([c, c_ctx[None, :]], axis=0)
    cmat = jnp.pad(cmat, ((0, MOD_ROWS - (b + 1)), (0, 0)))
    mod_all = _adaln(cmat, w_ada, b_ada)

    cos_t, sin_t = _rope_tables_t(n)
    cos_c = jnp.ones((HALF, nc), F32)
    sin_c = jnp.zeros((HALF, nc), F32)

    tm = _tile(n, TOKEN_TILE)
    tmc = _tile(nc, TOKEN_TILE)
    tq = _tile(n, Q_TILE)
    tk = _tile(n, K_TILE)
    reps = ATTN_W // QK_DIM

    for li in range(depth):
        last = li == depth - 1
        lam_init = 0.8 - 0.6 * math.exp(-0.3 * li)
        mods = [m.reshape(b, 1, d) for m in jnp.split(mod_all[li, :b], N_MOD, axis=-1)]
        cmods = [jnp.broadcast_to(m.reshape(1, 1, d), (b, 1, d))
                 for m in jnp.split(mod_all[li, b], N_MOD, axis=-1)]
        sh1, sc1, g1, sh2, sc2, g2 = mods
        csh1, csc1, cg1, csh2, csc2, cg2 = cmods

        w_att_t = w_in[li][:, :N_ATT].T.astype(BF16)
        w_rest = w_in[li][:, N_ATT:].astype(BF16)
        gcol = jnp.concatenate([jnp.tile(q_norm_g[li], reps) * (ATTN_SCALE * LOG2E),
                                jnp.tile(k_norm_g[li], reps)]).reshape(2 * ATTN_W, 1)
        lam_p = jnp.stack([lambda_q1[li], lambda_k1[li], lambda_q2[li], lambda_k2[li]])
        sub_g = subln_g[li].reshape(V_DIM, 1)
        n1 = norm1_g[li].reshape(1, d)
        n2 = norm2_g[li].reshape(1, d)
        wpa = w_pa[li].astype(BF16)
        wpc = w_pc[li].astype(BF16)
        wo = w_o[li].astype(BF16)
        wg = w_ffn_gate[li].astype(BF16)
        wu = w_ffn_up[li].astype(BF16)
        wd = w_ffn_down[li].astype(BF16)

        lat_t, rest = _proj(x, n1, sh1, sc1, w_att_t, gcol, cos_t, sin_t, w_rest, tm)
        ctx_t, rest_c = _proj(ctx, n1, csh1, csc1, w_att_t, gcol, cos_c, sin_c,
                              None if last else w_rest, tmc)

        attn = _attention(lat_t, ctx_t, lat_t, lam_p, sub_g, lam_init, tq, tk)
        x = _merge(attn, rest, conv_w[li], wpa, wpc, wo, x, g1, tm)
        x = _ffn(x, n2, sh2, sc2, g2, wg, wu, wd, tm)

        if not last:
            attn_c = _attention(ctx_t, ctx_t, None, lam_p, sub_g, lam_init, tmc, tk)
            ctx = _merge(attn_c, rest_c, conv_w[li], wpa, wpc, wo, ctx, cg1, tmc)
            ctx = _ffn(ctx, n2, csh2, csc2, cg2, wg, wu, wd, tmc)
    return x


def kernel(x, c, ctx, c_ctx, w_ada, b_ada, norm1_g, norm2_g, w_in, q_norm_g, k_norm_g,
           lambda_q1, lambda_k1, lambda_q2, lambda_k2, subln_g, conv_w, w_pa, w_pc, w_o,
           w_ffn_gate, w_ffn_up, w_ffn_down):
    return _forward(x, c, ctx, c_ctx, w_ada, b_ada, norm1_g, norm2_g, w_in, q_norm_g,
                    k_norm_g, lambda_q1, lambda_k1, lambda_q2, lambda_k2, subln_g, conv_w,
                    w_pa, w_pc, w_o, w_ffn_gate, w_ffn_up, w_ffn_down)
```

```python
import functools
import math

import jax
import jax.numpy as jnp
import numpy as np
from jax import lax
from jax.experimental import pallas as pl
from jax.experimental.pallas import tpu as pltpu

D_MODEL = 1024
N_HEADS = 8
QK_DIM = 64
HALF = QK_DIM // 2
V_DIM = 2 * QK_DIM
V_AUG = V_DIM + 16
ATTN_W = N_HEADS * V_DIM
N_ATT = 3 * ATTN_W
N_REST = 5 * D_MODEL
N_ROWS_OUT = 4 * D_MODEL
D_FF = 2816
GRID_W = 64
ROPE_BASE = 10000.0
ATTN_SCALE = QK_DIM ** -0.5
LOG2E = math.log2(math.e)
EPS = 1e-6
N_MOD = 6
MOD_ROWS = 8
HALO = 16
N_SUB = 2
TOKEN_TILE = 512
Q_TILE = 512
K_TILE = 512

BF16 = jnp.bfloat16
F32 = jnp.float32

VMEM_LIMIT = 56 * 1024 * 1024


def _params(sem):
    return pltpu.CompilerParams(dimension_semantics=sem, vmem_limit_bytes=VMEM_LIMIT)


def _sigmoid(x):
    return 1.0 / (1.0 + jnp.exp(-x))


def _norm_mod(x, g, shift, scale):
    ms = jnp.mean(x * x, axis=-1, keepdims=True)
    y = x * lax.rsqrt(ms + EPS) * g
    return y * (1.0 + scale) + shift


def _adaln_kernel(c_ref, w_ref, b_ref, o_ref):
    c = c_ref[...]
    s = (c * _sigmoid(c)).astype(BF16)
    w = w_ref[0].astype(BF16)
    o_ref[0] = jnp.dot(s, w, preferred_element_type=F32) + b_ref[0]


def _adaln(cmat, w_ada, b_ada):
    depth, d, n = w_ada.shape
    tn = 1536
    return pl.pallas_call(
        _adaln_kernel,
        out_shape=jax.ShapeDtypeStruct((depth, MOD_ROWS, n), F32),
        grid=(depth, n // tn),
        in_specs=[
            pl.BlockSpec((MOD_ROWS, d), lambda l, j: (0, 0)),
            pl.BlockSpec((1, d, tn), lambda l, j: (l, 0, j)),
            pl.BlockSpec((1, 1, tn), lambda l, j: (l, 0, j)),
        ],
        out_specs=pl.BlockSpec((1, MOD_ROWS, tn), lambda l, j: (l, 0, j)),
        compiler_params=_params(("arbitrary", "arbitrary")),
        name="adaln",
    )(cmat, w_ada, b_ada.reshape(depth, 1, n))


def _proj_kernel(*refs, with_rest):
    if with_rest:
        (x_ref, g_ref, sh_ref, sc_ref, wt_ref, gcol_ref, cos_ref, sin_ref, wr_ref,
         ot_ref, or_ref) = refs
    else:
        x_ref, g_ref, sh_ref, sc_ref, wt_ref, gcol_ref, cos_ref, sin_ref, ot_ref = refs
    c_all = cos_ref[...]
    s_all = sin_ref[...]
    tm = x_ref.shape[1]
    ts = tm // N_SUB
    for si in range(N_SUB):
        rows = slice(si * ts, (si + 1) * ts)
        h = _norm_mod(x_ref[0, rows, :], g_ref[...], sh_ref[0], sc_ref[0])
        hb = h.astype(BF16)

        def rest_dot(cb):
            return jnp.dot(hb, wr_ref[:, cb * D_MODEL:(cb + 1) * D_MODEL],
                           preferred_element_type=F32)

        def put(ob, val):
            or_ref[0, rows, ob * D_MODEL:(ob + 1) * D_MODEL] = val.astype(BF16)

        if with_rest:
            put(0, rest_dot(0) * rest_dot(2))
        ht = h.T.astype(BF16)
        c = c_all[:, rows]
        s = s_all[:, rows]
        ot_ref[0, 2 * ATTN_W:, rows] = jnp.dot(
            wt_ref[2 * ATTN_W:, :], ht, preferred_element_type=F32).astype(BF16)
        for j in range(2):
            y = jnp.dot(wt_ref[j * ATTN_W:(j + 1) * ATTN_W, :], ht,
                        preferred_element_type=F32)
            for gi in range(ATTN_W // QK_DIM):
                r0 = gi * QK_DIM
                blk = y[r0:r0 + QK_DIM, :]
                ms = jnp.mean(blk * blk, axis=0, keepdims=True)
                g0 = j * ATTN_W + r0
                blk = blk * lax.rsqrt(ms + EPS) * gcol_ref[g0:g0 + QK_DIM, :]
                t1 = blk[:HALF, :]
                t2 = blk[HALF:, :]
                ot_ref[0, g0:g0 + HALF, rows] = (t1 * c - t2 * s).astype(BF16)
                ot_ref[0, g0 + HALF:g0 + QK_DIM, rows] = (t2 * c + t1 * s).astype(BF16)
        if with_rest:
            put(1, rest_dot(1))
            put(2, _sigmoid(rest_dot(3)))
            put(3, _sigmoid(rest_dot(4)))


def _resident(shape):
    return pl.BlockSpec(shape, lambda *_: (0,) * len(shape), pipeline_mode=pl.Buffered(1))


def _proj(x, g, shift, scale, w_t, gcol, cos_t, sin_t, w_rest, tm):
    b, n, d = x.shape
    with_rest = w_rest is not None
    in_specs = [
        pl.BlockSpec((1, tm, d), lambda bi, i: (bi, i, 0)),
        pl.BlockSpec((1, d), lambda bi, i: (0, 0)),
        pl.BlockSpec((1, 1, d), lambda bi, i: (bi, 0, 0)),
        pl.BlockSpec((1, 1, d), lambda bi, i: (bi, 0, 0)),
        _resident((N_ATT, d)),
        pl.BlockSpec((2 * ATTN_W, 1), lambda bi, i: (0, 0)),
        pl.BlockSpec((HALF, tm), lambda bi, i: (0, i)),
        pl.BlockSpec((HALF, tm), lambda bi, i: (0, i)),
    ]
    args = [x, g, shift, scale, w_t, gcol, cos_t, sin_t]
    out_shape = [jax.ShapeDtypeStruct((b, N_ATT, n), BF16)]
    out_specs = [pl.BlockSpec((1, N_ATT, tm), lambda bi, i: (bi, 0, i))]
    if with_rest:
        in_specs.append(_resident((d, N_REST)))
        args.append(w_rest)
        out_shape.append(jax.ShapeDtypeStruct((b, n, N_ROWS_OUT), BF16))
        out_specs.append(pl.BlockSpec((1, tm, N_ROWS_OUT), lambda bi, i: (bi, i, 0)))
    outs = pl.pallas_call(
        functools.partial(_proj_kernel, with_rest=with_rest),
        out_shape=out_shape,
        grid=(b, n // tm),
        in_specs=in_specs,
        out_specs=out_specs,
        compiler_params=_params(("arbitrary", "arbitrary")),
        name="proj" if with_rest else "proj_qkv",
    )(*args)
    return (outs[0], outs[1]) if with_rest else (outs[0], None)


def _attn_kernel(*refs, lam_init, n_lat_blocks):
    nb = n_lat_blocks
    if nb:
        (q_ref, qn_ref, kc_ref, vc_ref, k_ref, v_ref, lam_ref, g_ref, o_ref,
         kc_rows, vc_aug, k_rows, v_aug, qm_ref, sc_buf, mbc_buf, s_buf, mb_buf,
         m_ref, acc_ref) = refs
    else:
        (q_ref, kc_ref, vc_ref, lam_ref, g_ref, o_ref,
         kc_rows, vc_aug, qm_ref, sc_buf, mbc_buf, m_ref, acc_ref) = refs
    qi = pl.program_id(2)
    par = qi % 2 if nb else 0

    def build_qm(slot, src_ref):
        qt = src_ref[0]
        row = lax.broadcasted_iota(jnp.int32, qt.shape, 0)
        zero = jnp.zeros_like(qt)
        qm_ref[slot, 0] = jnp.where(row < QK_DIM, qt, zero)
        qm_ref[slot, 1] = jnp.where(row >= QK_DIM, qt, zero)

    def produce_ctx(qslot):
        kcb = kc_rows[...]
        for mi in range(2):
            s = jnp.dot(kcb, qm_ref[qslot, mi], preferred_element_type=F32)
            sc_buf[mi] = s
            mbc_buf[mi] = jnp.max(s, axis=0, keepdims=True)

    def produce(slot, jb, qslot, maps=(0, 1)):
        kb = k_rows[jb]
        for mi in maps:
            s = jnp.dot(kb, qm_ref[qslot, mi], preferred_element_type=F32)
            s_buf[slot, mi] = s
            mb_buf[slot, mi] = jnp.max(s, axis=0, keepdims=True)

    def consume_ctx(maps=(0, 1)):
        vcb = vc_aug[...]
        for mi in maps:
            mb = mbc_buf[mi]
            m_ref[mi] = mb
            acc_ref[mi] = jnp.dot(vcb, jnp.exp2(sc_buf[mi] - mb).astype(BF16),
                                  preferred_element_type=F32)

    def consume(slot, jb, maps=(0, 1)):
        vtb = v_aug[jb]
        for mi in maps:
            m_old = m_ref[mi]
            m_new = jnp.maximum(m_old, mb_buf[slot, mi])
            alpha = jnp.exp2(m_old - m_new)
            p = jnp.exp2(s_buf[slot, mi] - m_new).astype(BF16)
            m_ref[mi] = m_new
            acc_ref[mi] = alpha * acc_ref[mi] + jnp.dot(vtb, p, preferred_element_type=F32)

    @pl.when(qi == 0)
    def _():
        def ones_rows(n):
            r = lax.broadcasted_iota(jnp.int32, (V_AUG - V_DIM, n), 0)
            return jnp.where(r == 0, 1.0, 0.0).astype(BF16)

        kc_rows[...] = kc_ref[0].T
        vc_aug[:V_DIM, :] = vc_ref[0]
        vc_aug[V_DIM:, :] = ones_rows(vc_aug.shape[1])
        for jb in range(nb):
            tk = k_rows.shape[1]
            k_rows[jb] = k_ref[0, :, jb * tk:(jb + 1) * tk].T
            v_aug[jb, :V_DIM, :] = v_ref[0, :, jb * tk:(jb + 1) * tk]
            v_aug[jb, V_DIM:, :] = ones_rows(tk)
        build_qm(0, q_ref)
        produce_ctx(0)
        if nb:
            produce(2, 0, 0)

    if nb:
        build_qm(1 - par, qn_ref)
    else:
        consume_ctx()

    if nb:
        def slot_of(jb):
            return 2 if jb == 0 else jb % 2

        def emit(jb):
            for mi in range(2):
                if jb + 1 < nb:
                    produce(slot_of(jb + 1), jb + 1, par, (mi,))
                if jb == 0:
                    consume_ctx((mi,))
                    if mi == 1:
                        consume(slot_of(0), 0)
                else:
                    consume(slot_of(jb), jb, (mi,))
                if jb + 1 == nb:
                    if mi == 0:
                        produce_ctx(1 - par)
                    else:
                        produce(2, 0, 1 - par)

        n_pre = 2 if nb >= 4 else 0
        n_pairs = (nb - n_pre - 2) // 2
        for jb in range(n_pre):
            emit(jb)
        if n_pairs:
            def body(pi, carry):
                j0 = n_pre + 2 * pi
                for mi in range(2):
                    produce(1, j0 + 1, par, (mi,))
                    consume(0, j0, (mi,))
                for mi in range(2):
                    produce(0, j0 + 2, par, (mi,))
                    consume(1, j0 + 1, (mi,))
                return carry
            unroll = 3 if n_pairs % 3 == 0 else (2 if n_pairs % 2 == 0 else 1)
            lax.fori_loop(0, n_pairs, body, 0, unroll=unroll)
        for jb in range(n_pre + 2 * n_pairs, nb):
            emit(jb)

    lp = lam_ref[...]
    lam = (jnp.exp(jnp.sum(lp[0:1] * lp[1:2], axis=-1, keepdims=True))
           - jnp.exp(jnp.sum(lp[2:3] * lp[3:4], axis=-1, keepdims=True)) + lam_init)
    a1 = acc_ref[0]
    a2 = acc_ref[1]
    o = (a1[:V_DIM] / a1[V_DIM:V_DIM + 1]
         - lam * (a2[:V_DIM] / a2[V_DIM:V_DIM + 1]))
    ms = jnp.mean(o * o, axis=0, keepdims=True)
    o = o * lax.rsqrt(ms + EPS) * g_ref[...] * (1.0 - lam_init)
    o_ref[0] = o.T.astype(BF16)


def _attention(q_t, ctx_t, lat_t, lam_p, g_col, lam_init, tq, tk):
    b, _, nq = q_t.shape
    nc = ctx_t.shape[2]
    h = N_HEADS
    n_lat_blocks = 0 if lat_t is None else lat_t.shape[2] // tk
    n_q = nq // tq
    in_specs = [pl.BlockSpec((1, V_DIM, tq), lambda bi, hi, qi: (bi, hi, qi))]
    args = [q_t]
    if n_lat_blocks:
        in_specs.append(pl.BlockSpec(
            (1, V_DIM, tq), lambda bi, hi, qi: (bi, hi, jnp.minimum(qi + 1, n_q - 1))))
        args.append(q_t)
    in_specs += [
        pl.BlockSpec((1, V_DIM, nc), lambda bi, hi, qi: (bi, h + hi, 0)),
        pl.BlockSpec((1, V_DIM, nc), lambda bi, hi, qi: (bi, 2 * h + hi, 0)),
    ]
    args += [ctx_t, ctx_t]
    scratch = [pltpu.VMEM((nc, V_DIM), BF16), pltpu.VMEM((V_AUG, nc), BF16)]
    if n_lat_blocks:
        assert n_lat_blocks % 2 == 0, "latent key tiles are consumed in pairs"
        nl = lat_t.shape[2]
        in_specs += [
            pl.BlockSpec((1, V_DIM, nl), lambda bi, hi, qi: (bi, h + hi, 0)),
            pl.BlockSpec((1, V_DIM, nl), lambda bi, hi, qi: (bi, 2 * h + hi, 0)),
        ]
        args += [lat_t, lat_t]
        scratch += [pltpu.VMEM((n_lat_blocks, tk, V_DIM), BF16),
                    pltpu.VMEM((n_lat_blocks, V_AUG, tk), BF16)]
    in_specs += [
        pl.BlockSpec((4, QK_DIM), lambda bi, hi, qi: (0, 0)),
        pl.BlockSpec((V_DIM, 1), lambda bi, hi, qi: (0, 0)),
    ]
    args += [lam_p, g_col]
    scratch += [pltpu.VMEM((2, 2, V_DIM, tq), BF16),
                pltpu.VMEM((2, nc, tq), F32), pltpu.VMEM((2, 1, tq), F32)]
    if n_lat_blocks:
        scratch += [pltpu.VMEM((3, 2, tk, tq), F32), pltpu.VMEM((3, 2, 1, tq), F32)]
    scratch += [pltpu.VMEM((2, 1, tq), F32), pltpu.VMEM((2, V_AUG, tq), F32)]
    return pl.pallas_call(
        functools.partial(_attn_kernel, lam_init=lam_init, n_lat_blocks=n_lat_blocks),
        out_shape=jax.ShapeDtypeStruct((b, nq, ATTN_W), BF16),
        grid=(b, h, n_q),
        in_specs=in_specs,
        out_specs=pl.BlockSpec((1, tq, V_DIM), lambda bi, hi, qi: (bi, qi, hi)),
        scratch_shapes=scratch,
        compiler_params=_params(("arbitrary", "arbitrary", "arbitrary")),
        name="diff_attn" if n_lat_blocks else "diff_attn_ctx",
    )(*args)


def _merge_kernel(attn_ref, u_ref, cb_ref, sa_ref, sc_ref, up_ref, un_ref, cw_ref,
                  wpa_ref, wpc_ref, wo_ref, x_ref, g1_ref, o_ref):
    i = pl.program_id(1)
    last = pl.num_programs(1) - 1
    tm = x_ref.shape[1]
    u = u_ref[0].astype(F32)
    prev = jnp.where(i == 0, 0.0, up_ref[0, HALO - 1:HALO, :].astype(F32))
    nxt = jnp.where(i == last, 0.0, un_ref[0, 0:1, :].astype(F32))
    rid = lax.broadcasted_iota(jnp.int32, u.shape, 0)
    u_prev = jnp.where(rid == 0, prev, pltpu.roll(u, 1, 0))
    u_next = jnp.where(rid == tm - 1, nxt, pltpu.roll(u, tm - 1, 0))
    cw = cw_ref[...]
    ts = tm // N_SUB
    subs = [slice(si * ts, (si + 1) * ts) for si in range(N_SUB)]
    ya = [jnp.dot(attn_ref[0, r, :], wpa_ref[...], preferred_element_type=F32) for r in subs]
    conv = u_prev * cw[0:1] + u * cw[1:2] + u_next * cw[2:3]
    y_conv = (cb_ref[0].astype(F32) * conv).astype(BF16)
    yc = [jnp.dot(y_conv[r], wpc_ref[...], preferred_element_type=F32) for r in subs]
    for si, r in enumerate(subs):
        z = sa_ref[0, r, :].astype(F32) * ya[si] + sc_ref[0, r, :].astype(F32) * yc[si]
        mix = jnp.dot(z.astype(BF16), wo_ref[...], preferred_element_type=F32)
        o_ref[0, r, :] = x_ref[0, r, :] + g1_ref[0] * mix


def _merge(attn, rest, conv_w, w_pa, w_pc, w_o, x, g1, tm):
    b, n, d = x.shape
    nh = n // HALO
    th = tm // HALO
    col = lambda c: pl.BlockSpec((1, tm, d), lambda bi, i, c=c: (bi, i, c))
    prev = lambda c: pl.BlockSpec(
        (1, HALO, d), lambda bi, i, c=c: (bi, jnp.maximum(i * th - 1, 0), c))
    nxt = lambda c: pl.BlockSpec(
        (1, HALO, d), lambda bi, i, c=c: (bi, jnp.minimum((i + 1) * th, nh - 1), c))
    return pl.pallas_call(
        _merge_kernel,
        out_shape=jax.ShapeDtypeStruct((b, n, d), F32),
        grid=(b, n // tm),
        in_specs=[
            pl.BlockSpec((1, tm, d), lambda bi, i: (bi, i, 0)),
            col(0), col(1), col(2), col(3),
            prev(0), nxt(0),
            pl.BlockSpec((3, d), lambda bi, i: (0, 0)),
            _resident((d, d)), _resident((d, d)), _resident((d, d)),
            pl.BlockSpec((1, tm, d), lambda bi, i: (bi, i, 0)),
            pl.BlockSpec((1, 1, d), lambda bi, i: (bi, 0, 0)),
        ],
        out_specs=pl.BlockSpec((1, tm, d), lambda bi, i: (bi, i, 0)),
        compiler_params=_params(("arbitrary", "arbitrary")),
        name="merge",
    )(attn, rest, rest, rest, rest, rest, rest, conv_w, w_pa, w_pc, w_o, x, g1)


FF_CHUNKS = ((0, 1536), (1536, 1280))


def _ffn_kernel(x_ref, g_ref, sh_ref, sc_ref, g2_ref, wg_ref, wu_ref, wd_ref, o_ref):
    ts = x_ref.shape[1] // N_SUB
    for si in range(N_SUB):
        rows = slice(si * ts, (si + 1) * ts)
        x = x_ref[0, rows, :]
        h = _norm_mod(x, g_ref[...], sh_ref[0], sc_ref[0]).astype(BF16)
        acc = None
        for c0, cw in FF_CHUNKS:
            gate = jnp.dot(h, wg_ref[:, c0:c0 + cw], preferred_element_type=F32)
            up = jnp.dot(h, wu_ref[:, c0:c0 + cw], preferred_element_type=F32)
            a = (gate * _sigmoid(gate) * up).astype(BF16)
            part = jnp.dot(a, wd_ref[c0:c0 + cw, :], preferred_element_type=F32)
            acc = part if acc is None else acc + part
        o_ref[0, rows, :] = x + g2_ref[0] * acc


def _ffn(x, g, shift, scale, g2, wg, wu, wd, tm):
    b, n, d = x.shape
    dff = wg.shape[1]
    assert sum(cw for _, cw in FF_CHUNKS) == dff
    vec = pl.BlockSpec((1, 1, d), lambda bi, i: (bi, 0, 0))
    return pl.pallas_call(
        _ffn_kernel,
        out_shape=jax.ShapeDtypeStruct((b, n, d), F32),
        grid=(b, n // tm),
        in_specs=[
            pl.BlockSpec((1, tm, d), lambda bi, i: (bi, i, 0)),
            pl.BlockSpec((1, d), lambda bi, i: (0, 0)),
            vec, vec, vec,
            _resident((d, dff)), _resident((d, dff)), _resident((dff, d)),
        ],
        out_specs=pl.BlockSpec((1, tm, d), lambda bi, i: (bi, i, 0)),
        compiler_params=_params(("arbitrary", "arbitrary")),
        name="ffn",
    )(x, g, shift, scale, g2, wg, wu, wd)


def _rope_tables_t(n):
    pos = np.arange(n)
    row = (pos // GRID_W).astype(np.float64)
    col = (pos % GRID_W).astype(np.float64)
    pairs = HALF // 2
    inv = ROPE_BASE ** (-np.arange(pairs, dtype=np.float64) / pairs)
    ang = np.concatenate([inv[:, None] * row[None, :], inv[:, None] * col[None, :]], axis=0)
    return jnp.asarray(np.cos(ang), F32), jnp.asarray(np.sin(ang), F32)


def _tile(n, pref):
    return pref if n % pref == 0 else n


@jax.jit
def _forward(x, c, ctx, c_ctx, w_ada, b_ada, norm1_g, norm2_g, w_in, q_norm_g, k_norm_g,
             lambda_q1, lambda_k1, lambda_q2, lambda_k2, subln_g, conv_w, w_pa, w_pc, w_o,
             w_ffn_gate, w_ffn_up, w_ffn_down):
    b, n, d = x.shape
    nc = ctx.shape[1]
    depth = w_ada.shape[0]

    cmat = jnp.concatenate([c, c_ctx[None, :]], axis=0)
    cmat = jnp.pad(cmat, ((0, MOD_ROWS - (b + 1)), (0, 0)))
    mod_all = _adaln(cmat, w_ada, b_ada)

    cos_t, sin_t = _rope_tables_t(n)
    cos_c = jnp.ones((HALF, nc), F32)
    sin_c = jnp.zeros((HALF, nc), F32)

    tm = _tile(n, TOKEN_TILE)
    tmc = _tile(nc, TOKEN_TILE)
    tq = _tile(n, Q_TILE)
    tk = _tile(n, K_TILE)
    reps = ATTN_W // QK_DIM

    for li in range(depth):
        last = li == depth - 1
        lam_init = 0.8 - 0.6 * math.exp(-0.3 * li)
        mods = [m.reshape(b, 1, d) for m in jnp.split(mod_all[li, :b], N_MOD, axis=-1)]
        cmods = [jnp.broadcast_to(m.reshape(1, 1, d), (b, 1, d))
                 for m in jnp.split(mod_all[li, b], N_MOD, axis=-1)]
        sh1, sc1, g1, sh2, sc2, g2 = mods
        csh1, csc1, cg1, csh2, csc2, cg2 = cmods

        w_att_t = w_in[li][:, :N_ATT].T.astype(BF16)
        w_rest = w_in[li][:, N_ATT:].astype(BF16)
        gcol = jnp.concatenate([jnp.tile(q_norm_g[li], reps) * (ATTN_SCALE * LOG2E),
                                jnp.tile(k_norm_g[li], reps)]).reshape(2 * ATTN_W, 1)
        lam_p = jnp.stack([lambda_q1[li], lambda_k1[li], lambda_q2[li], lambda_k2[li]])
        sub_g = subln_g[li].reshape(V_DIM, 1)
        n1 = norm1_g[li].reshape(1, d)
        n2 = norm2_g[li].reshape(1, d)
        wpa = w_pa[li].astype(BF16)
        wpc = w_pc[li].astype(BF16)
        wo = w_o[li].astype(BF16)
        wg = w_ffn_gate[li].astype(BF16)
        wu = w_ffn_up[li].astype(BF16)
        wd = w_ffn_down[li].astype(BF16)

        lat_t, rest = _proj(x, n1, sh1, sc1, w_att_t, gcol, cos_t, sin_t, w_rest, tm)
        ctx_t, rest_c = _proj(ctx, n1, csh1, csc1, w_att_t, gcol, cos_c, sin_c,
                              None if last else w_rest, tmc)

        attn = _attention(lat_t, ctx_t, lat_t, lam_p, sub_g, lam_init, tq, tk)
        x = _merge(attn, rest, conv_w[li], wpa, wpc, wo, x, g1, tm)
        x = _ffn(x, n2, sh2, sc2, g2, wg, wu, wd, tm)

        if not last:
            attn_c = _attention(ctx_t, ctx_t, None, lam_p, sub_g, lam_init, tmc, tk)
            ctx = _merge(attn_c, rest_c, conv_w[li], wpa, wpc, wo, ctx, cg1, tmc)
            ctx = _ffn(ctx, n2, csh2, csc2, cg2, wg, wu, wd, tmc)
    return x


def kernel(x, c, ctx, c_ctx, w_ada, b_ada, norm1_g, norm2_g, w_in, q_norm_g, k_norm_g,
           lambda_q1, lambda_k1, lambda_q2, lambda_k2, subln_g, conv_w, w_pa, w_pc, w_o,
           w_ffn_gate, w_ffn_up, w_ffn_down):
    return _forward(x, c, ctx, c_ctx, w_ada, b_ada, norm1_g, norm2_g, w_in, q_norm_g,
                    k_norm_g, lambda_q1, lambda_k1, lambda_q2, lambda_k2, subln_g, conv_w,
                    w_pa, w_pc, w_o, w_ffn_gate, w_ffn_up, w_ffn_down)
```
